```python
import math
import jax, jax.numpy as jnp
from jax import lax
import numpy as np

D_MODEL = 1024
BATCH = 2
SEQ = 16384
DEPTH = 4

HEAD_DIM = 64
BLOCK = 128
EPS = 1e-6
NEG = -1e30
A_Q_HEADS = 8
A_KV_HEADS = 2
A_GROUP = A_Q_HEADS // A_KV_HEADS
A_WINDOW = 128
B_BRANCHES = ((128, 1), (512, 4), (2048, 16))
B_HEADS_PER_BRANCH = 4
B_HEADS = len(B_BRANCHES) * B_HEADS_PER_BRANCH
NUM_BUCKETS = 32
MAX_DISTANCE = 2048
N_BIAS_HEADS = A_Q_HEADS + B_HEADS
A_IN = (A_Q_HEADS + 2 * A_KV_HEADS) * HEAD_DIM
B_IN = 3 * B_HEADS * HEAD_DIM
AB_IN = A_IN + B_IN
AB_OUT = (A_Q_HEADS + B_HEADS_PER_BRANCH) * HEAD_DIM
C_HEADS = 8
C_NOPE = 64
C_ROPE = 32
C_V = 64
C_Q_RANK = 384
C_KV_RANK = 256
C_DOWN = C_Q_RANK + C_KV_RANK + C_ROPE
ROPE_THETA = 10000.0
D_FF = 4 * D_MODEL
N_EVEN = (DEPTH + 1) // 2
N_ODD = DEPTH // 2

kernel_name = 'hybrid_swa_dilated_mla_trunk'


def rmsnorm(x, g):
    xf = x.astype(jnp.float32)
    y = xf * lax.rsqrt(jnp.mean(xf * xf, axis=-1, keepdims=True) + EPS)
    return (y * g.astype(jnp.float32)).astype(x.dtype)


def t5_bucket(n):
    max_exact = NUM_BUCKETS // 2
    nf = jnp.maximum(n, 1).astype(jnp.float32)
    large = max_exact + (jnp.log(nf / max_exact) / math.log(MAX_DISTANCE / max_exact)
                         * (NUM_BUCKETS - max_exact)).astype(jnp.int32)
    return jnp.where(n < max_exact, n, jnp.minimum(large, NUM_BUCKETS - 1))


def band_bias(table, dilation, max_dist):
    i = jnp.arange(BLOCK)[:, None]
    j = jnp.arange(2 * BLOCK)[None, :]
    dist = i + BLOCK - j
    inband = (dist >= 0) & (dist <= max_dist)
    bias = table[t5_bucket(jnp.maximum(dist, 0) * dilation)].astype(jnp.float32)
    bias = jnp.where(inband[..., None], bias, NEG)
    return bias.transpose(2, 0, 1)


def banded_attention(q, k, v, bias, sinks=None, with_lse=False):
    Bn, L, KVH, G, dh = q.shape
    n = L // BLOCK
    qb = q.reshape(Bn, n, BLOCK, KVH, G, dh)

    def two_blocks(t):
        cur = t.reshape(Bn, n, BLOCK, KVH, dh)
        prev = jnp.pad(cur, ((0, 0), (1, 0), (0, 0), (0, 0), (0, 0)))[:, :-1]
        return jnp.concatenate([prev, cur], axis=2)

    kb, vb = two_blocks(k), two_blocks(v)
    logits = jnp.einsum('bnqhgd,bnkhd->bnhgqk', qb, kb).astype(jnp.float32) * (dh ** -0.5)
    logits = logits + bias.reshape(KVH, G, BLOCK, 2 * BLOCK)
    first = (jnp.arange(n)[:, None] == 0) & (jnp.arange(2 * BLOCK)[None, :] < BLOCK)
    logits = jnp.where(first[None, :, None, None, None, :], NEG, logits)
    m = logits.max(axis=-1)
    if sinks is not None:
        s = sinks.astype(jnp.float32).reshape(KVH, G)[None, None, :, :, None]
        m = jnp.maximum(m, s)
    p = jnp.exp(logits - m[..., None])
    l = p.sum(axis=-1)
    if sinks is not None:
        l = l + jnp.exp(s - m)
    acc = jnp.einsum('bnhgqk,bnkhd->bnqhgd', p.astype(v.dtype), vb)
    o = (acc / l.transpose(0, 1, 4, 2, 3)[..., None].astype(v.dtype)).reshape(Bn, L, KVH, G, dh)
    if with_lse:
        lse = (m + jnp.log(l)).transpose(0, 1, 4, 2, 3).reshape(Bn, L, KVH, G)
        return o, lse
    return o


def dilate(t, d):
    Bn, S = t.shape[:2]
    L = S // d
    t = jnp.moveaxis(t.reshape((Bn, L, d) + t.shape[2:]), 2, 1).reshape((Bn * d, L) + t.shape[2:])
    pad = (-L) % BLOCK
    return jnp.pad(t, [(0, 0), (0, pad)] + [(0, 0)] * (t.ndim - 2))


def undilate(t, Bn, S, d):
    L = S // d
    t = t[:, :L].reshape((Bn, d, L) + t.shape[2:])
    return jnp.moveaxis(t, 1, 2).reshape((Bn, S) + t.shape[3:])


def even_mixer(xn, w_in, sinks, w_out, bias_a, bias_b):
    Bn, S, _ = xn.shape
    proj = xn @ w_in
    pa, pb = proj[..., :A_IN], proj[..., A_IN:]
    qd, kd = A_Q_HEADS * HEAD_DIM, A_KV_HEADS * HEAD_DIM
    qa = pa[..., :qd].reshape(Bn, S, A_KV_HEADS, A_GROUP, HEAD_DIM)
    ka = pa[..., qd:qd + kd].reshape(Bn, S, A_KV_HEADS, HEAD_DIM)
    va = pa[..., qd + kd:].reshape(Bn, S, A_KV_HEADS, HEAD_DIM)
    out_a = banded_attention(qa, ka, va, bias_a, sinks=sinks).reshape(Bn, S, qd)

    qkv_b = pb.reshape(Bn, S, len(B_BRANCHES), 3, B_HEADS_PER_BRANCH, HEAD_DIM)
    outs, lses = [], []
    for g, (window, dil) in enumerate(B_BRANCHES):
        q, k, v = (dilate(qkv_b[:, :, g, i], dil) for i in range(3))
        o, lse = banded_attention(q[:, :, :, None, :], k, v, bias_b[g], with_lse=True)
        outs.append(undilate(o[:, :, :, 0], Bn, S, dil))
        lses.append(undilate(lse[..., 0], Bn, S, dil))
    wts = jax.nn.softmax(jnp.stack(lses), axis=0)
    out_b = jnp.einsum('gbsh,gbshd->bshd', wts.astype(xn.dtype), jnp.stack(outs)).reshape(Bn, S, -1)
    return jnp.concatenate([out_a, out_b], axis=-1) @ w_out


def rope(t):
    S, r = t.shape[1], t.shape[-1]
    inv = ROPE_THETA ** (-jnp.arange(0, r, 2, dtype=jnp.float32) / r)
    ang = jnp.arange(S, dtype=jnp.float32)[:, None] * inv[None, :]
    shape = (1, S) + (1,) * (t.ndim - 3) + (r // 2,)
    cos, sin = jnp.cos(ang).reshape(shape), jnp.sin(ang).reshape(shape)
    t1, t2 = t[..., :r // 2].astype(jnp.float32), t[..., r // 2:].astype(jnp.float32)
    return jnp.concatenate([t1 * cos - t2 * sin, t1 * sin + t2 * cos], axis=-1).astype(t.dtype)


def causal_mla_attention(q_nope, q_rope, k_nope, k_rope, v):
    Bn, S, H, _ = q_nope.shape
    n = S // BLOCK
    scale = (C_NOPE + C_ROPE) ** -0.5
    kpos = jnp.arange(S)

    def block(args):
        qn, qr, i = args
        logits = (jnp.einsum('bqhd,bkhd->bhqk', qn, k_nope)
                  + jnp.einsum('bqhr,bkr->bhqk', qr, k_rope)).astype(jnp.float32) * scale
        qpos = i * BLOCK + jnp.arange(BLOCK)
        logits = jnp.where(kpos[None, :] <= qpos[:, None], logits, NEG)
        p = jax.nn.softmax(logits, axis=-1).astype(v.dtype)
        return jnp.einsum('bhqk,bkhd->bqhd', p, v)

    qn_b = q_nope.reshape(Bn, n, BLOCK, H, C_NOPE).transpose(1, 0, 2, 3, 4)
    qr_b = q_rope.reshape(Bn, n, BLOCK, H, C_ROPE).transpose(1, 0, 2, 3, 4)
    out = lax.map(block, (qn_b, qr_b, jnp.arange(n)))
    return out.transpose(1, 0, 2, 3, 4).reshape(Bn, S, H * C_V)


def mla_mixer(xn, w_down, q_norm, w_uq, kv_norm, w_ukv, w_o):
    Bn, S, _ = xn.shape
    down = xn @ w_down
    c_q = down[..., :C_Q_RANK]
    c_kv = down[..., C_Q_RANK:C_Q_RANK + C_KV_RANK]
    k_rope = rope(down[..., C_Q_RANK + C_KV_RANK:])
    q = (rmsnorm(c_q, q_norm) @ w_uq).reshape(Bn, S, C_HEADS, C_NOPE + C_ROPE)
    kv = (rmsnorm(c_kv, kv_norm) @ w_ukv).reshape(Bn, S, C_HEADS, C_NOPE + C_V)
    out = causal_mla_attention(q[..., :C_NOPE], rope(q[..., C_NOPE:]),
                               kv[..., :C_NOPE], k_rope, kv[..., C_NOPE:])
    return out @ w_o


def squared_relu_mlp(xn, w_up, w_down):
    h = jax.nn.relu(xn @ w_up)
    return (h * h) @ w_down


def setup_inputs(seed: int = 0) -> dict:
    key = jax.random.key(seed)
    ks = jax.random.split(key, 16)

    def w(k, shape, fan_in):
        return jax.random.normal(k, shape, jnp.float32) * fan_in ** -0.5

    def gain(k, shape):
        return 1.0 + 0.05 * jax.random.normal(k, shape, jnp.float32)

    return {
        'x': jax.random.normal(ks[0], (BATCH, SEQ, D_MODEL), jnp.float32),
        'rel_bias': 0.5 * jax.random.normal(ks[1], (NUM_BUCKETS, N_BIAS_HEADS), jnp.float32),
        'attn_norm': gain(ks[2], (DEPTH, D_MODEL)),
        'mlp_norm': gain(ks[3], (DEPTH, D_MODEL)),
        'final_norm': gain(ks[4], (D_MODEL,)),
        'w_in_ab': w(ks[5], (N_EVEN, D_MODEL, AB_IN), D_MODEL),
        'sinks': jax.random.normal(ks[6], (N_EVEN, A_Q_HEADS), jnp.float32),
        'w_out_ab': w(ks[7], (N_EVEN, AB_OUT, D_MODEL), AB_OUT),
        'w_down_c': w(ks[8], (N_ODD, D_MODEL, C_DOWN), D_MODEL),
        'q_norm_c': gain(ks[9], (N_ODD, C_Q_RANK)),
        'w_uq_c': w(ks[10], (N_ODD, C_Q_RANK, C_HEADS * (C_NOPE + C_ROPE)), C_Q_RANK),
        'kv_norm_c': gain(ks[11], (N_ODD, C_KV_RANK)),
        'w_ukv_c': w(ks[12], (N_ODD, C_KV_RANK, C_HEADS * (C_NOPE + C_V)), C_KV_RANK),
        'w_o_c': w(ks[13], (N_ODD, C_HEADS * C_V, D_MODEL), C_HEADS * C_V),
        'w_mlp_up': w(ks[14], (DEPTH, D_MODEL, D_FF), D_MODEL),
        'w_mlp_down': w(ks[15], (DEPTH, D_FF, D_MODEL), D_FF),
    }


def reference(x, rel_bias, attn_norm, mlp_norm, final_norm, w_in_ab, sinks, w_out_ab,
              w_down_c, q_norm_c, w_uq_c, kv_norm_c, w_ukv_c, w_o_c, w_mlp_up, w_mlp_down):
    bias_a = band_bias(rel_bias[:, :A_Q_HEADS], 1, A_WINDOW - 1)
    bias_b = [band_bias(rel_bias[:, A_Q_HEADS + g * B_HEADS_PER_BRANCH:A_Q_HEADS + (g + 1) * B_HEADS_PER_BRANCH],
                        dil, window // dil)
              for g, (window, dil) in enumerate(B_BRANCHES)]
    h = x
    for layer in range(DEPTH):
        xn = rmsnorm(h, attn_norm[layer])
        if layer % 2 == 0:
            e = layer // 2
            mix = even_mixer(xn, w_in_ab[e], sinks[e], w_out_ab[e], bias_a, bias_b)
        else:
            o = layer // 2
            mix = mla_mixer(xn, w_down_c[o], q_norm_c[o], w_uq_c[o], kv_norm_c[o], w_ukv_c[o], w_o_c[o])
        h = h + mix
        h = h + squared_relu_mlp(rmsnorm(h, mlp_norm[layer]), w_mlp_up[layer], w_mlp_down[layer])
    return rmsnorm(h, final_norm)
```

```python
import functools
import math

import numpy as np
import jax
import jax.numpy as jnp
from jax import lax
from jax.experimental import pallas as pl
from jax.experimental.pallas import tpu as pltpu

F32 = jnp.float32
BF16 = jnp.bfloat16

D_MODEL = 1024
HEAD_DIM = 64
BLOCK = 128
EPS = 1e-6
NEG = -1e30
A_Q_HEADS = 8
A_KV_HEADS = 2
A_GROUP = A_Q_HEADS // A_KV_HEADS
A_WINDOW = 128
B_BRANCHES = ((128, 1), (512, 4), (2048, 16))
B_HEADS_PER_BRANCH = 4
NUM_BUCKETS = 32
MAX_DISTANCE = 2048
N_BIAS_HEADS = A_Q_HEADS + len(B_BRANCHES) * B_HEADS_PER_BRANCH
A_IN = (A_Q_HEADS + 2 * A_KV_HEADS) * HEAD_DIM
B_BRANCH_IN = 3 * B_HEADS_PER_BRANCH * HEAD_DIM
AB_IN = A_IN + len(B_BRANCHES) * B_BRANCH_IN
A_OUT = A_Q_HEADS * HEAD_DIM
B_OUT = B_HEADS_PER_BRANCH * HEAD_DIM
C_HEADS = 8
C_NOPE = 64
C_ROPE = 32
C_V = 64
C_Q_RANK = 384
C_KV_RANK = 256
ROPE_THETA = 10000.0
D_FF = 4 * D_MODEL

LANES = 128
C_PAD = C_HEADS * LANES
C_DOWN_EXT = C_Q_RANK + C_KV_RANK + LANES
ROPE_LO = C_NOPE
ROPE_HALF = C_ROPE // 2

VMEM_LIMIT = 52 * 1024 * 1024

TM_PROJ = 512
TM_POST = 256
T_ATT = 256

_NT = (((1,), (1,)), ((), ()))


def _const_spec(shape):
    nd = len(shape)
    return pl.BlockSpec(shape, lambda *_: (0,) * nd, pipeline_mode=pl.Buffered(1))


def _rms(x, g):
    ms = jnp.mean(x * x, axis=-1, keepdims=True)
    return x * lax.rsqrt(ms + EPS) * g


def _t5_bucket_np(n):
    max_exact = NUM_BUCKETS // 2
    nf = np.maximum(n, 1).astype(np.float32)
    large = max_exact + (np.log(nf / max_exact) / math.log(MAX_DISTANCE / max_exact)
                         * (NUM_BUCKETS - max_exact)).astype(np.int32)
    return np.where(n < max_exact, n, np.minimum(large, NUM_BUCKETS - 1))


def _band_bucket_index(dilation, max_dist):
    i = np.arange(BLOCK)[:, None]
    j = np.arange(2 * BLOCK)[None, :]
    dist = i + BLOCK - j
    inband = (dist >= 0) & (dist <= max_dist)
    return np.where(inband, _t5_bucket_np(np.maximum(dist, 0) * dilation), -1).astype(np.int32)


def _bias_kernel(tab_ref, idx_ref, o_ref):
    h = pl.program_id(0)
    idx = idx_ref[0]
    acc = jnp.full(idx.shape, NEG, F32)
    for b in range(NUM_BUCKETS):
        acc = jnp.where(idx == b, tab_ref[h * NUM_BUCKETS + b], acc)
    o_ref[0] = acc


def _bias_table(rel_bias):
    configs = [(1, A_WINDOW - 1)] + [(dil, window // dil) for window, dil in B_BRANCHES]
    idx = jnp.asarray(np.stack([_band_bucket_index(d, m) for d, m in configs]))
    tab = rel_bias.T.reshape(-1)

    def cfg(h):
        return jnp.where(h < A_Q_HEADS, 0, 1 + (h - A_Q_HEADS) // B_HEADS_PER_BRANCH)

    return pl.pallas_call(
        _bias_kernel,
        grid=(N_BIAS_HEADS,),
        in_specs=[pl.BlockSpec(memory_space=pltpu.SMEM),
                  pl.BlockSpec((1, BLOCK, 2 * BLOCK), lambda h: (cfg(h), 0, 0))],
        out_specs=pl.BlockSpec((1, BLOCK, 2 * BLOCK), lambda h: (h, 0, 0)),
        out_shape=jax.ShapeDtypeStruct((N_BIAS_HEADS, BLOCK, 2 * BLOCK), F32),
        name="bias_table",
    )(tab, idx)


def _norm_proj_kernel(x_ref, g_ref, w_ref, cs_ref, o_ref):
    xn = _rms(x_ref[...], g_ref[...])
    y = jnp.dot(xn.astype(BF16), w_ref[...], preferred_element_type=F32)
    o_ref[...] = (y * cs_ref[...]).astype(o_ref.dtype)


def _norm_proj(h, g, w, colscale):
    n, d = h.shape
    dout = w.shape[1]
    return pl.pallas_call(
        _norm_proj_kernel,
        grid=(n // TM_PROJ,),
        in_specs=[pl.BlockSpec((TM_PROJ, d), lambda i: (i, 0)),
                  _const_spec((1, d)), _const_spec((d, dout)), _const_spec((1, dout))],
        out_specs=pl.BlockSpec((TM_PROJ, dout), lambda i: (i, 0)),
        out_shape=jax.ShapeDtypeStruct((n, dout), BF16),
        compiler_params=pltpu.CompilerParams(dimension_semantics=("arbitrary",),
                                             vmem_limit_bytes=VMEM_LIMIT),
        name="norm_proj",
    )(h, g, w, colscale)


def _band_head(q, k, v, bias, first_mask, sink):
    s = lax.dot_general(q, k, _NT, preferred_element_type=F32) + bias
    s = jnp.where(first_mask, NEG, s)
    m = jnp.max(s, axis=-1, keepdims=True)
    if sink is not None:
        m = jnp.maximum(m, sink)
    p = jnp.exp(s - m)
    l = jnp.sum(p, axis=-1, keepdims=True)
    if sink is not None:
        l = l + jnp.exp(sink - m)
    o = jnp.dot(p.astype(BF16), v, preferred_element_type=F32)
    return o / l, m, l


def _first_block_mask(block_idx):
    col = lax.broadcasted_iota(jnp.int32, (BLOCK, 2 * BLOCK), 1)
    return jnp.logical_and(block_idx == 0, col < BLOCK)


def _band_a_kernel(sink_ref, q_ref, kc_ref, kp_ref, vc_ref, vp_ref, bias_ref, o_ref):
    first_mask = _first_block_mask(pl.program_id(1))
    q = q_ref[0]
    k = jnp.concatenate([kp_ref[0], kc_ref[0]], axis=0)
    v = jnp.concatenate([vp_ref[0], vc_ref[0]], axis=0)
    outs = []
    for h in range(A_Q_HEADS):
        j = h // A_GROUP
        o, _, _ = _band_head(q[:, h * HEAD_DIM:(h + 1) * HEAD_DIM],
                             k[:, j * HEAD_DIM:(j + 1) * HEAD_DIM],
                             v[:, j * HEAD_DIM:(j + 1) * HEAD_DIM],
                             bias_ref[h], first_mask, sink_ref[h])
        outs.append(o)
    o_ref[0] = jnp.concatenate(outs, axis=-1).astype(o_ref.dtype)


def _band_a(proj, sinks, bias_all):
    bn, s, _ = proj.shape
    kd = A_KV_HEADS * HEAD_DIM
    k_col = A_OUT // kd
    v_col = k_col + 1
    prev = lambda n: jnp.maximum(n - 1, 0)
    return pl.pallas_call(
        _band_a_kernel,
        grid=(bn, s // BLOCK),
        in_specs=[pl.BlockSpec(memory_space=pltpu.SMEM),
                  pl.BlockSpec((1, BLOCK, A_OUT), lambda b, n: (b, n, 0)),
                  pl.BlockSpec((1, BLOCK, kd), lambda b, n: (b, n, k_col)),
                  pl.BlockSpec((1, BLOCK, kd), lambda b, n: (b, prev(n), k_col)),
                  pl.BlockSpec((1, BLOCK, kd), lambda b, n: (b, n, v_col)),
                  pl.BlockSpec((1, BLOCK, kd), lambda b, n: (b, prev(n), v_col)),
                  pl.BlockSpec((A_Q_HEADS, BLOCK, 2 * BLOCK), lambda b, n: (0, 0, 0))],
        out_specs=pl.BlockSpec((1, BLOCK, A_OUT), lambda b, n: (b, n, 0)),
        out_shape=jax.ShapeDtypeStruct((bn, s, A_OUT), BF16),
        compiler_params=pltpu.CompilerParams(dimension_semantics=("arbitrary", "arbitrary")),
        name="band_a",
    )(sinks, proj, proj, proj, proj, proj, bias_all)


def _band_b_kernel(q_ref, kc_ref, kp_ref, vc_ref, vp_ref, bias_ref, o_ref):
    first_mask = _first_block_mask(pl.program_id(2))
    q = q_ref[0]
    k = jnp.concatenate([kp_ref[0], kc_ref[0]], axis=0)
    v = jnp.concatenate([vp_ref[0], vc_ref[0]], axis=0)
    outs, lses = [], []
    for h in range(B_HEADS_PER_BRANCH):
        sl = slice(h * HEAD_DIM, (h + 1) * HEAD_DIM)
        o, m, l = _band_head(q[:, sl], k[:, sl], v[:, sl], bias_ref[h], first_mask, None)
        outs.append(o)
        lses.append(jnp.broadcast_to(m + jnp.log(l), (BLOCK, HEAD_DIM)))
    o_ref[0] = jnp.concatenate(outs + lses, axis=-1)


def _band_b(proj, bias_all, g, dil):
    bn, s, _ = proj.shape
    l = s // dil
    view = proj.reshape(bn, l, dil * AB_IN)
    per_res = AB_IN // B_OUT
    q_col = A_IN // B_OUT + g * (B_BRANCH_IN // B_OUT)
    prev = lambda n: jnp.maximum(n - 1, 0)

    def spec(off, use_prev):
        row = prev if use_prev else (lambda n: n)
        return pl.BlockSpec((1, BLOCK, B_OUT), lambda b, r, n: (b, row(n), r * per_res + q_col + off))

    bias_blk = (A_Q_HEADS + g * B_HEADS_PER_BRANCH) // B_HEADS_PER_BRANCH
    out = pl.pallas_call(
        _band_b_kernel,
        grid=(bn, dil, l // BLOCK),
        in_specs=[spec(0, False), spec(1, False), spec(1, True), spec(2, False), spec(2, True),
                  pl.BlockSpec((B_HEADS_PER_BRANCH, BLOCK, 2 * BLOCK), lambda b, r, n: (bias_blk, 0, 0))],
        out_specs=pl.BlockSpec((1, BLOCK, 2 * B_OUT), lambda b, r, n: (b, n, r)),
        out_shape=jax.ShapeDtypeStruct((bn, l, dil * 2 * B_OUT), F32),
        compiler_params=pltpu.CompilerParams(
            dimension_semantics=("arbitrary", "arbitrary", "arbitrary")),
        name=f"band_b{g}",
    )(view, view, view, view, view, bias_all)
    return out.reshape(bn, s, 2 * B_OUT)


def _mlp_tail(h2, gm_ref, wu_ref, wd_ref, gf_ref, o_ref):
    xn = _rms(h2, gm_ref[...]).astype(BF16)
    u = jnp.maximum(jnp.dot(xn, wu_ref[...], preferred_element_type=F32), 0.0)
    u = (u * u).astype(BF16)
    h3 = h2 + jnp.dot(u, wd_ref[...], preferred_element_type=F32)
    if gf_ref is not None:
        h3 = _rms(h3, gf_ref[...])
    o_ref[...] = h3


def _post_even_kernel(final, h_ref, oa_ref, b0_ref, b1_ref, b2_ref, wo_ref, gm_ref, wu_ref, wd_ref,
                      *rest):
    gf_ref, o_ref = (rest[0], rest[1]) if final else (None, rest[0])
    obs = [b0_ref[...], b1_ref[...], b2_ref[...]]
    lses = [ob[:, B_OUT:] for ob in obs]
    mx = jnp.maximum(jnp.maximum(lses[0], lses[1]), lses[2])
    es = [jnp.exp(ls - mx) for ls in lses]
    num = es[0] * obs[0][:, :B_OUT] + es[1] * obs[1][:, :B_OUT] + es[2] * obs[2][:, :B_OUT]
    out_b = num / (es[0] + es[1] + es[2])
    mix = jnp.dot(oa_ref[...], wo_ref[:A_OUT, :], preferred_element_type=F32)
    mix = mix + jnp.dot(out_b.astype(BF16), wo_ref[A_OUT:, :], preferred_element_type=F32)
    _mlp_tail(h_ref[...] + mix, gm_ref, wu_ref, wd_ref, gf_ref, o_ref)


def _post_odd_kernel(final, h_ref, a_ref, wo_ref, gm_ref, wu_ref, wd_ref, *rest):
    gf_ref, o_ref = (rest[0], rest[1]) if final else (None, rest[0])
    mix = jnp.dot(a_ref[...], wo_ref[...], preferred_element_type=F32)
    _mlp_tail(h_ref[...] + mix, gm_ref, wu_ref, wd_ref, gf_ref, o_ref)


def _post(kernel_fn, name, h, acts, wo, gm, wu, wd, gf):
    n, d = h.shape
    final = gf is not None
    tile = lambda a: pl.BlockSpec((TM_POST, a.shape[1]), lambda i: (i, 0))
    params = [wo, gm, wu, wd] + ([gf] if final else [])
    return pl.pallas_call(
        functools.partial(kernel_fn, final),
        grid=(n // TM_POST,),
        in_specs=[tile(h)] + [tile(a) for a in acts] + [_const_spec(p.shape) for p in params],
        out_specs=pl.BlockSpec((TM_POST, d), lambda i: (i, 0)),
        out_shape=jax.ShapeDtypeStruct((n, d), F32),
        compiler_params=pltpu.CompilerParams(dimension_semantics=("arbitrary",),
                                             vmem_limit_bytes=VMEM_LIMIT),
        name=name,
    )(h, *acts, *params)


def _rope_lanes(x, cos, s1, s2):
    return x * cos + pltpu.roll(x, LANES - ROPE_HALF, 1) * s1 + pltpu.roll(x, ROPE_HALF, 1) * s2


def _mla_proj_kernel(scale, h_ref, g_ref, wd_ref, qn_ref, wq_ref, kn_ref, wk_ref, wv_ref,
                     cos_ref, s1_ref, s2_ref, q_ref, k_ref, v_ref):
    xn = _rms(h_ref[...], g_ref[...]).astype(BF16)
    down = jnp.dot(xn, wd_ref[...], preferred_element_type=F32)
    cq = _rms(down[:, :C_Q_RANK], qn_ref[...]).astype(BF16)
    ckv = _rms(down[:, C_Q_RANK:C_Q_RANK + C_KV_RANK], kn_ref[...]).astype(BF16)
    cos, s1, s2 = cos_ref[...], s1_ref[...], s2_ref[...]
    kr = _rope_lanes(down[:, C_Q_RANK + C_KV_RANK:], cos, s1, s2)
    q = jnp.dot(cq, wq_ref[...], preferred_element_type=F32)
    k = jnp.dot(ckv, wk_ref[...], preferred_element_type=F32)
    lane = lax.broadcasted_iota(jnp.int32, (1, LANES), 1)
    cos_q = jnp.where(lane < C_NOPE, 1.0, cos)
    for hh in range(C_HEADS):
        sl = slice(hh * LANES, (hh + 1) * LANES)
        q_ref[:, sl] = (_rope_lanes(q[:, sl], cos_q, s1, s2) * scale).astype(q_ref.dtype)
        k_ref[:, sl] = (k[:, sl] + kr).astype(k_ref.dtype)
    v_ref[...] = jnp.dot(ckv, wv_ref[...], preferred_element_type=F32).astype(v_ref.dtype)


def _rope_tables(s):
    inv = ROPE_THETA ** (-jnp.arange(0, C_ROPE, 2, dtype=F32) / C_ROPE)
    ang = jnp.arange(s, dtype=F32)[:, None] * inv[None, :]
    cos, sin = jnp.cos(ang), jnp.sin(ang)
    z = lambda w: jnp.zeros((s, w), F32)
    tail = LANES - ROPE_LO - C_ROPE
    cos_t = jnp.concatenate([z(ROPE_LO), cos, cos, z(tail)], axis=1)
    s1_t = jnp.concatenate([z(ROPE_LO), -sin, z(ROPE_HALF), z(tail)], axis=1)
    s2_t = jnp.concatenate([z(ROPE_LO), z(ROPE_HALF), sin, z(tail)], axis=1)
    return cos_t, s1_t, s2_t


def _pad_heads(w, width, offset_fn):
    r = w.shape[0]
    w3 = w.reshape(r, C_HEADS, width)
    out = jnp.zeros((r, C_HEADS, LANES), w.dtype)
    for hh in range(C_HEADS):
        off = offset_fn(hh)
        out = out.at[:, hh, off:off + width].set(w3[:, hh])
    return out.reshape(r, C_PAD)


def _mla_weights(w_down, w_uq, w_ukv):
    kr = jnp.zeros((D_MODEL, LANES), w_down.dtype).at[:, ROPE_LO:ROPE_LO + C_ROPE].set(
        w_down[:, C_Q_RANK + C_KV_RANK:])
    wd_ext = jnp.concatenate([w_down[:, :C_Q_RANK + C_KV_RANK], kr], axis=1)
    wq = _pad_heads(w_uq, C_NOPE + C_ROPE, lambda hh: 0)
    ukv = w_ukv.reshape(C_KV_RANK, C_HEADS, C_NOPE + C_V)
    wk = _pad_heads(ukv[:, :, :C_NOPE].reshape(C_KV_RANK, -1), C_NOPE, lambda hh: 0)
    wv = _pad_heads(ukv[:, :, C_NOPE:].reshape(C_KV_RANK, -1), C_V, lambda hh: (hh % 2) * C_V)
    return wd_ext.astype(BF16), wq.astype(BF16), wk.astype(BF16), wv.astype(BF16)


def _mla_proj(h, g, wd_ext, qn, wq, kn, wk, wv, tables, seq):
    n, d = h.shape
    scale = (C_NOPE + C_ROPE) ** -0.5
    per_seq = seq // TM_PROJ
    tab_spec = pl.BlockSpec((TM_PROJ, LANES), lambda i: (i % per_seq, 0))
    tile = lambda w: pl.BlockSpec((TM_PROJ, w), lambda i: (i, 0))
    params = [g, wd_ext, qn, wq, kn, wk, wv]
    return pl.pallas_call(
        functools.partial(_mla_proj_kernel, scale),
        grid=(n // TM_PROJ,),
        in_specs=[tile(d)] + [_const_spec(p.shape) for p in params] + [tab_spec] * 3,
        out_specs=[tile(C_PAD), tile(C_PAD), tile(C_PAD)],
        out_shape=[jax.ShapeDtypeStruct((n, C_PAD), BF16)] * 3,
        compiler_params=pltpu.CompilerParams(dimension_semantics=("arbitrary",),
                                             vmem_limit_bytes=VMEM_LIMIT),
        name="mla_proj",
    )(h, *params, *tables)


def _mla_attn_kernel(q_ref, k_ref, v_ref, o_ref):
    i = pl.program_id(2)
    t = T_ATT
    q0 = q_ref[0, :, :LANES]
    q1 = q_ref[0, :, LANES:]
    lane = lax.broadcasted_iota(jnp.int32, (1, LANES), 1)
    left = lane < C_V

    def step(off, carry, mask):
        m0, l0, m1, l1, acc = carry
        kv_rows = pl.ds(off, t)
        k = k_ref[0, kv_rows, :]
        v = v_ref[0, kv_rows, :]
        s0 = lax.dot_general(q0, k[:, :LANES], _NT, preferred_element_type=F32)
        s1 = lax.dot_general(q1, k[:, LANES:], _NT, preferred_element_type=F32)
        if mask is not None:
            s0 = jnp.where(mask, s0, NEG)
            s1 = jnp.where(mask, s1, NEG)
        n0 = jnp.maximum(m0, jnp.max(s0, axis=-1, keepdims=True))
        n1 = jnp.maximum(m1, jnp.max(s1, axis=-1, keepdims=True))
        a0 = jnp.exp(m0 - n0)
        a1 = jnp.exp(m1 - n1)
        p0 = jnp.exp(s0 - n0)
        p1 = jnp.exp(s1 - n1)
        l0 = a0 * l0 + jnp.sum(p0, axis=-1, keepdims=True)
        l1 = a1 * l1 + jnp.sum(p1, axis=-1, keepdims=True)
        p = jnp.concatenate([p0.astype(BF16), p1.astype(BF16)], axis=1)
        vv = jnp.concatenate([v[:, :LANES], v[:, LANES:]], axis=0)
        acc = jnp.where(left, a0, a1) * acc + jnp.dot(p, vv, preferred_element_type=F32)
        return n0, l0, n1, l1, acc

    col0 = jnp.full((t, 1), NEG, F32)
    zero = jnp.zeros((t, 1), F32)
    init = (col0, zero, col0, zero, jnp.zeros((t, LANES), F32))
    carry = lax.fori_loop(
        0, i, lambda j, c: step(pl.multiple_of(j * t, t), c, None), init)
    row = lax.broadcasted_iota(jnp.int32, (t, t), 0)
    col = lax.broadcasted_iota(jnp.int32, (t, t), 1)
    _, l0, _, l1, acc = step(pl.multiple_of(i * t, t), carry, col <= row)
    o_ref[0] = (acc / jnp.where(left, l0, l1)).astype(o_ref.dtype)


def _mla_attn(q, k, v):
    bn, s, _ = q.shape
    pair = 2 * LANES
    return pl.pallas_call(
        _mla_attn_kernel,
        grid=(bn, C_HEADS // 2, s // T_ATT),
        in_specs=[pl.BlockSpec((1, T_ATT, pair), lambda b, p, i: (b, i, p)),
                  pl.BlockSpec((1, s, pair), lambda b, p, i: (b, 0, p)),
                  pl.BlockSpec((1, s, pair), lambda b, p, i: (b, 0, p))],
        out_specs=pl.BlockSpec((1, T_ATT, LANES), lambda b, p, i: (b, i, p)),
        out_shape=jax.ShapeDtypeStruct((bn, s, C_HEADS * C_V), BF16),
        compiler_params=pltpu.CompilerParams(
            dimension_semantics=("arbitrary", "arbitrary", "arbitrary"),
            vmem_limit_bytes=VMEM_LIMIT),
        name="mla_attn",
    )(q, k, v)


def _even_colscale():
    cs = np.ones((1, AB_IN), np.float32)
    cs[:, :A_OUT] = HEAD_DIM ** -0.5
    for g in range(len(B_BRANCHES)):
        lo = A_IN + g * B_BRANCH_IN
        cs[:, lo:lo + B_OUT] = HEAD_DIM ** -0.5
    return jnp.asarray(cs)


def kernel(x, rel_bias, attn_norm, mlp_norm, final_norm, w_in_ab, sinks, w_out_ab, w_down_c, q_norm_c,
           w_uq_c, kv_norm_c, w_ukv_c, w_o_c, w_mlp_up, w_mlp_down):
    bn, s, d = x.shape
    depth = attn_norm.shape[0]
    n = bn * s
    bias_all = _bias_table(rel_bias)
    colscale = _even_colscale()
    tables = _rope_tables(s)
    row = lambda v: v.reshape(1, -1)
    h = x.reshape(n, d)
    for layer in range(depth):
        gf = row(final_norm) if layer == depth - 1 else None
        gm, wu, wd = row(mlp_norm[layer]), w_mlp_up[layer].astype(BF16), w_mlp_down[layer].astype(BF16)
        if layer % 2 == 0:
            e = layer // 2
            proj = _norm_proj(h, row(attn_norm[layer]), w_in_ab[e].astype(BF16), colscale)
            proj = proj.reshape(bn, s, AB_IN)
            oa = _band_a(proj, sinks[e], bias_all).reshape(n, A_OUT)
            obs = [_band_b(proj, bias_all, g, dil).reshape(n, 2 * B_OUT)
                   for g, (_, dil) in enumerate(B_BRANCHES)]
            h = _post(_post_even_kernel, "post_even", h, [oa] + obs, w_out_ab[e].astype(BF16),
                      gm, wu, wd, gf)
        else:
            o = layer // 2
            wd_ext, wq, wk, wv = _mla_weights(w_down_c[o], w_uq_c[o], w_ukv_c[o])
            q, k, v = _mla_proj(h, row(attn_norm[layer]), wd_ext, row(q_norm_c[o]), wq,
                                row(kv_norm_c[o]), wk, wv, tables, s)
            att = _mla_attn(q.reshape(bn, s, C_PAD), k.reshape(bn, s, C_PAD), v.reshape(bn, s, C_PAD))
            h = _post(_post_odd_kernel, "post_odd", h, [att.reshape(n, C_HEADS * C_V)],
                      w_o_c[o].astype(BF16), gm, wu, wd, gf)
    return h.reshape(bn, s, d)
```

```python
import functools
import math

import numpy as np
import jax
import jax.numpy as jnp
from jax import lax
from jax.experimental import pallas as pl
from jax.experimental.pallas import tpu as pltpu

F32 = jnp.float32
BF16 = jnp.bfloat16

D_MODEL = 1024
HEAD_DIM = 64
BLOCK = 128
EPS = 1e-6
NEG = -1e30
A_Q_HEADS = 8
A_KV_HEADS = 2
A_GROUP = A_Q_HEADS // A_KV_HEADS
A_WINDOW = 128
B_BRANCHES = ((128, 1), (512, 4), (2048, 16))
B_HEADS_PER_BRANCH = 4
NUM_BUCKETS = 32
MAX_DISTANCE = 2048
N_BIAS_HEADS = A_Q_HEADS + len(B_BRANCHES) * B_HEADS_PER_BRANCH
A_IN = (A_Q_HEADS + 2 * A_KV_HEADS) * HEAD_DIM
B_BRANCH_IN = 3 * B_HEADS_PER_BRANCH * HEAD_DIM
AB_IN = A_IN + len(B_BRANCHES) * B_BRANCH_IN
A_OUT = A_Q_HEADS * HEAD_DIM
B_OUT = B_HEADS_PER_BRANCH * HEAD_DIM
C_HEADS = 8
C_NOPE = 64
C_ROPE = 32
C_V = 64
C_Q_RANK = 384
C_KV_RANK = 256
ROPE_THETA = 10000.0
D_FF = 4 * D_MODEL

LANES = 128
BF16_ROWS = 16
C_PAD = C_HEADS * LANES
C_DOWN_EXT = C_Q_RANK + C_KV_RANK + LANES
ROPE_LO = C_NOPE
ROPE_HALF = C_ROPE // 2

VMEM_LIMIT = 52 * 1024 * 1024

TM_PROJ = 512
TM_POST = 256
T_ATT = 512
TK_ATT = 256
assert T_ATT == 2 * TK_ATT

_NT = (((1,), (1,)), ((), ()))


def _const_spec(shape):
    nd = len(shape)
    return pl.BlockSpec(shape, lambda *_: (0,) * nd, pipeline_mode=pl.Buffered(1))


def _rms(x, g):
    ms = jnp.mean(x * x, axis=-1, keepdims=True)
    return x * lax.rsqrt(ms + EPS) * g


def _t5_bucket_np(n):
    max_exact = NUM_BUCKETS // 2
    nf = np.maximum(n, 1).astype(np.float32)
    large = max_exact + (np.log(nf / max_exact) / math.log(MAX_DISTANCE / max_exact)
                         * (NUM_BUCKETS - max_exact)).astype(np.int32)
    return np.where(n < max_exact, n, np.minimum(large, NUM_BUCKETS - 1))


def _band_bucket_index(dilation, max_dist):
    i = np.arange(BLOCK)[:, None]
    j = np.arange(2 * BLOCK)[None, :]
    dist = i + BLOCK - j
    inband = (dist >= 0) & (dist <= max_dist)
    return np.where(inband, _t5_bucket_np(np.maximum(dist, 0) * dilation), -1).astype(np.int32)


def _bias_kernel(tab_ref, idx_ref, o_ref):
    h = pl.program_id(0)
    idx = idx_ref[0]
    acc = jnp.full(idx.shape, NEG, F32)
    for b in range(NUM_BUCKETS):
        acc = jnp.where(idx == b, tab_ref[h * NUM_BUCKETS + b], acc)
    o_ref[0] = acc


def _bias_table(rel_bias):
    configs = [(1, A_WINDOW - 1)] + [(dil, window // dil) for window, dil in B_BRANCHES]
    idx = jnp.asarray(np.stack([_band_bucket_index(d, m) for d, m in configs]))
    tab = rel_bias.T.reshape(-1)

    def cfg(h):
        return jnp.where(h < A_Q_HEADS, 0, 1 + (h - A_Q_HEADS) // B_HEADS_PER_BRANCH)

    return pl.pallas_call(
        _bias_kernel,
        grid=(N_BIAS_HEADS,),
        in_specs=[pl.BlockSpec(memory_space=pltpu.SMEM),
                  pl.BlockSpec((1, BLOCK, 2 * BLOCK), lambda h: (cfg(h), 0, 0))],
        out_specs=pl.BlockSpec((1, BLOCK, 2 * BLOCK), lambda h: (h, 0, 0)),
        out_shape=jax.ShapeDtypeStruct((N_BIAS_HEADS, BLOCK, 2 * BLOCK), F32),
        name="bias_table",
    )(tab, idx)


def _norm_proj_kernel(x_ref, g_ref, w_ref, cs_ref, o_ref):
    xn = _rms(x_ref[...], g_ref[...])
    y = jnp.dot(xn.astype(BF16), w_ref[...], preferred_element_type=F32)
    o_ref[...] = (y * cs_ref[...]).astype(o_ref.dtype)


def _norm_proj(h, g, w, colscale):
    n, d = h.shape
    dout = w.shape[1]
    return pl.pallas_call(
        _norm_proj_kernel,
        grid=(n // TM_PROJ,),
        in_specs=[pl.BlockSpec((TM_PROJ, d), lambda i: (i, 0)),
                  _const_spec((1, d)), _const_spec((d, dout)), _const_spec((1, dout))],
        out_specs=pl.BlockSpec((TM_PROJ, dout), lambda i: (i, 0)),
        out_shape=jax.ShapeDtypeStruct((n, dout), BF16),
        compiler_params=pltpu.CompilerParams(dimension_semantics=("arbitrary",),
                                             vmem_limit_bytes=VMEM_LIMIT),
        name="norm_proj",
    )(h, g, w, colscale)


def _band_head(q, k, v, bias, first_mask, sink):
    s = lax.dot_general(q, k, _NT, preferred_element_type=F32) + bias
    s = jnp.where(first_mask, NEG, s)
    m = jnp.max(s, axis=-1, keepdims=True)
    if sink is not None:
        m = jnp.maximum(m, sink)
    p = jnp.exp(s - m)
    l = jnp.sum(p, axis=-1, keepdims=True)
    if sink is not None:
        l = l + jnp.exp(sink - m)
    o = jnp.dot(p.astype(BF16), v, preferred_element_type=F32)
    return o / l, m, l


def _first_block_mask(block_idx):
    col = lax.broadcasted_iota(jnp.int32, (BLOCK, 2 * BLOCK), 1)
    return jnp.logical_and(block_idx == 0, col < BLOCK)


def _band_a_kernel(sink_ref, q_ref, kc_ref, kp_ref, vc_ref, vp_ref, bias_ref, o_ref):
    first_mask = _first_block_mask(pl.program_id(1))
    q = q_ref[0]
    k = jnp.concatenate([kp_ref[0], kc_ref[0]], axis=0)
    v = jnp.concatenate([vp_ref[0], vc_ref[0]], axis=0)
    outs = []
    for h in range(A_Q_HEADS):
        j = h // A_GROUP
        o, _, _ = _band_head(q[:, h * HEAD_DIM:(h + 1) * HEAD_DIM],
                             k[:, j * HEAD_DIM:(j + 1) * HEAD_DIM],
                             v[:, j * HEAD_DIM:(j + 1) * HEAD_DIM],
                             bias_ref[h], first_mask, sink_ref[h])
        outs.append(o)
    o_ref[0] = jnp.concatenate(outs, axis=-1).astype(o_ref.dtype)


def _band_a(proj, sinks, bias_all):
    bn, s, _ = proj.shape
    kd = A_KV_HEADS * HEAD_DIM
    k_col = A_OUT // kd
    v_col = k_col + 1
    prev = lambda n: jnp.maximum(n - 1, 0)
    return pl.pallas_call(
        _band_a_kernel,
        grid=(bn, s // BLOCK),
        in_specs=[pl.BlockSpec(memory_space=pltpu.SMEM),
                  pl.BlockSpec((1, BLOCK, A_OUT), lambda b, n: (b, n, 0)),
                  pl.BlockSpec((1, BLOCK, kd), lambda b, n: (b, n, k_col)),
                  pl.BlockSpec((1, BLOCK, kd), lambda b, n: (b, prev(n), k_col)),
                  pl.BlockSpec((1, BLOCK, kd), lambda b, n: (b, n, v_col)),
                  pl.BlockSpec((1, BLOCK, kd), lambda b, n: (b, prev(n), v_col)),
                  pl.BlockSpec((A_Q_HEADS, BLOCK, 2 * BLOCK), lambda b, n: (0, 0, 0))],
        out_specs=pl.BlockSpec((1, BLOCK, A_OUT), lambda b, n: (b, n, 0)),
        out_shape=jax.ShapeDtypeStruct((bn, s, A_OUT), BF16),
        compiler_params=pltpu.CompilerParams(dimension_semantics=("arbitrary", "arbitrary")),
        name="band_a",
    )(sinks, proj, proj, proj, proj, proj, bias_all)


def _band_b_kernel(q_ref, kc_ref, kp_ref, vc_ref, vp_ref, bias_ref, o_ref):
    first_mask = _first_block_mask(pl.program_id(2))
    q = q_ref[0]
    k = jnp.concatenate([kp_ref[0], kc_ref[0]], axis=0)
    v = jnp.concatenate([vp_ref[0], vc_ref[0]], axis=0)
    outs, lses = [], []
    for h in range(B_HEADS_PER_BRANCH):
        sl = slice(h * HEAD_DIM, (h + 1) * HEAD_DIM)
        o, m, l = _band_head(q[:, sl], k[:, sl], v[:, sl], bias_ref[h], first_mask, None)
        outs.append(o)
        lses.append(jnp.broadcast_to(m + jnp.log(l), (BLOCK, HEAD_DIM)))
    o_ref[0] = jnp.concatenate(outs + lses, axis=-1)


def _band_b(proj, bias_all, g, dil):
    bn, s, _ = proj.shape
    l = s // dil
    view = proj.reshape(bn, l, dil * AB_IN)
    per_res = AB_IN // B_OUT
    q_col = A_IN // B_OUT + g * (B_BRANCH_IN // B_OUT)
    prev = lambda n: jnp.maximum(n - 1, 0)

    def spec(off, use_prev):
        row = prev if use_prev else (lambda n: n)
        return pl.BlockSpec((1, BLOCK, B_OUT), lambda b, r, n: (b, row(n), r * per_res + q_col + off))

    bias_blk = (A_Q_HEADS + g * B_HEADS_PER_BRANCH) // B_HEADS_PER_BRANCH
    out = pl.pallas_call(
        _band_b_kernel,
        grid=(bn, dil, l // BLOCK),
        in_specs=[spec(0, False), spec(1, False), spec(1, True), spec(2, False), spec(2, True),
                  pl.BlockSpec((B_HEADS_PER_BRANCH, BLOCK, 2 * BLOCK), lambda b, r, n: (bias_blk, 0, 0))],
        out_specs=pl.BlockSpec((1, BLOCK, 2 * B_OUT), lambda b, r, n: (b, n, r)),
        out_shape=jax.ShapeDtypeStruct((bn, l, dil * 2 * B_OUT), F32),
        compiler_params=pltpu.CompilerParams(
            dimension_semantics=("arbitrary", "arbitrary", "arbitrary")),
        name=f"band_b{g}",
    )(view, view, view, view, view, bias_all)
    return out.reshape(bn, s, 2 * B_OUT)


def _mlp_tail(h2, gm_ref, wu_ref, wd_ref, gf_ref, o_ref):
    xn = _rms(h2, gm_ref[...]).astype(BF16)
    u = jnp.maximum(jnp.dot(xn, wu_ref[...], preferred_element_type=F32), 0.0)
    u = (u * u).astype(BF16)
    h3 = h2 + jnp.dot(u, wd_ref[...], preferred_element_type=F32)
    if gf_ref is not None:
        h3 = _rms(h3, gf_ref[...])
    o_ref[...] = h3


def _post_even_kernel(final, h_ref, oa_ref, b0_ref, b1_ref, b2_ref, wo_ref, gm_ref, wu_ref, wd_ref,
                      *rest):
    gf_ref, o_ref = (rest[0], rest[1]) if final else (None, rest[0])
    obs = [b0_ref[...], b1_ref[...], b2_ref[...]]
    lses = [ob[:, B_OUT:] for ob in obs]
    mx = jnp.maximum(jnp.maximum(lses[0], lses[1]), lses[2])
    es = [jnp.exp(ls - mx) for ls in lses]
    num = es[0] * obs[0][:, :B_OUT] + es[1] * obs[1][:, :B_OUT] + es[2] * obs[2][:, :B_OUT]
    out_b = num / (es[0] + es[1] + es[2])
    mix = jnp.dot(oa_ref[...], wo_ref[:A_OUT, :], preferred_element_type=F32)
    mix = mix + jnp.dot(out_b.astype(BF16), wo_ref[A_OUT:, :], preferred_element_type=F32)
    _mlp_tail(h_ref[...] + mix, gm_ref, wu_ref, wd_ref, gf_ref, o_ref)


def _post_odd_kernel(final, h_ref, a_ref, wo_ref, gm_ref, wu_ref, wd_ref, *rest):
    gf_ref, o_ref = (rest[0], rest[1]) if final else (None, rest[0])
    mix = jnp.dot(a_ref[...], wo_ref[...], preferred_element_type=F32)
    _mlp_tail(h_ref[...] + mix, gm_ref, wu_ref, wd_ref, gf_ref, o_ref)


def _post(kernel_fn, name, h, acts, wo, gm, wu, wd, gf):
    n, d = h.shape
    final = gf is not None
    tile = lambda a: pl.BlockSpec((TM_POST, a.shape[1]), lambda i: (i, 0))
    params = [wo, gm, wu, wd] + ([gf] if final else [])
    return pl.pallas_call(
        functools.partial(kernel_fn, final),
        grid=(n // TM_POST,),
        in_specs=[tile(h)] + [tile(a) for a in acts] + [_const_spec(p.shape) for p in params],
        out_specs=pl.BlockSpec((TM_POST, d), lambda i: (i, 0)),
        out_shape=jax.ShapeDtypeStruct((n, d), F32),
        compiler_params=pltpu.CompilerParams(dimension_semantics=("arbitrary",),
                                             vmem_limit_bytes=VMEM_LIMIT),
        name=name,
    )(h, *acts, *params)


def _rope_lanes(x, cos, s1, s2):
    return x * cos + pltpu.roll(x, LANES - ROPE_HALF, 1) * s1 + pltpu.roll(x, ROPE_HALF, 1) * s2


def _mla_proj_kernel(scale, h_ref, g_ref, wd_ref, qn_ref, wq_ref, kn_ref, wk_ref, wv_ref,
                     cos_ref, s1_ref, s2_ref, q_ref, k_ref, vt_ref):
    xn = _rms(h_ref[...], g_ref[...]).astype(BF16)
    down = jnp.dot(xn, wd_ref[...], preferred_element_type=F32)
    cq = _rms(down[:, :C_Q_RANK], qn_ref[...]).astype(BF16)
    ckv = _rms(down[:, C_Q_RANK:C_Q_RANK + C_KV_RANK], kn_ref[...]).astype(BF16)
    cos, s1, s2 = cos_ref[...], s1_ref[...], s2_ref[...]
    kr = _rope_lanes(down[:, C_Q_RANK + C_KV_RANK:], cos, s1, s2)
    q = jnp.dot(cq, wq_ref[...], preferred_element_type=F32)
    k = jnp.dot(ckv, wk_ref[...], preferred_element_type=F32)
    lane = lax.broadcasted_iota(jnp.int32, (1, LANES), 1)
    cos_q = jnp.where(lane < C_NOPE, 1.0, cos)
    for hh in range(C_HEADS):
        sl = slice(hh * LANES, (hh + 1) * LANES)
        q_ref[:, sl] = (_rope_lanes(q[:, sl], cos_q, s1, s2) * scale).astype(q_ref.dtype)
        k_ref[:, sl] = (k[:, sl] + kr).astype(k_ref.dtype)
    vt_ref[0] = lax.dot_general(wv_ref[...], ckv, _NT, preferred_element_type=F32).astype(vt_ref.dtype)


def _rope_tables(s):
    inv = ROPE_THETA ** (-jnp.arange(0, C_ROPE, 2, dtype=F32) / C_ROPE)
    ang = jnp.arange(s, dtype=F32)[:, None] * inv[None, :]
    cos, sin = jnp.cos(ang), jnp.sin(ang)
    z = lambda w: jnp.zeros((s, w), F32)
    tail = LANES - ROPE_LO - C_ROPE
    cos_t = jnp.concatenate([z(ROPE_LO), cos, cos, z(tail)], axis=1)
    s1_t = jnp.concatenate([z(ROPE_LO), -sin, z(ROPE_HALF), z(tail)], axis=1)
    s2_t = jnp.concatenate([z(ROPE_LO), z(ROPE_HALF), sin, z(tail)], axis=1)
    return cos_t, s1_t, s2_t


def _pad_heads(w, width):
    r = w.shape[0]
    w3 = jnp.pad(w.reshape(r, C_HEADS, width), ((0, 0), (0, 0), (0, LANES - width)))
    return w3.reshape(r, C_PAD)


def _mla_weights(w_down, w_uq, w_ukv):
    kr = jnp.zeros((D_MODEL, LANES), w_down.dtype).at[:, ROPE_LO:ROPE_LO + C_ROPE].set(
        w_down[:, C_Q_RANK + C_KV_RANK:])
    wd_ext = jnp.concatenate([w_down[:, :C_Q_RANK + C_KV_RANK], kr], axis=1)
    wq = _pad_heads(w_uq, C_NOPE + C_ROPE)
    ukv = w_ukv.reshape(C_KV_RANK, C_HEADS, C_NOPE + C_V)
    wk = _pad_heads(ukv[:, :, :C_NOPE].reshape(C_KV_RANK, -1), C_NOPE)
    wvt = ukv[:, :, C_NOPE:].reshape(C_KV_RANK, -1).T
    return wd_ext.astype(BF16), wq.astype(BF16), wk.astype(BF16), wvt.astype(BF16)


def _mla_proj(h, g, wd_ext, qn, wq, kn, wk, wvt, tables, seq):
    n, d = h.shape
    scale = (C_NOPE + C_ROPE) ** -0.5 * math.log2(math.e)
    per_seq = seq // TM_PROJ
    tab_spec = pl.BlockSpec((TM_PROJ, LANES), lambda i: (i % per_seq, 0))
    tile = lambda w: pl.BlockSpec((TM_PROJ, w), lambda i: (i, 0))
    vt_rows = C_HEADS * C_V
    vt_spec = pl.BlockSpec((1, vt_rows, TM_PROJ), lambda i: (i // per_seq, 0, i % per_seq))
    params = [g, wd_ext, qn, wq, kn, wk, wvt]
    return pl.pallas_call(
        functools.partial(_mla_proj_kernel, scale),
        grid=(n // TM_PROJ,),
        in_specs=[tile(d)] + [_const_spec(p.shape) for p in params] + [tab_spec] * 3,
        out_specs=[tile(C_PAD), tile(C_PAD), vt_spec],
        out_shape=[jax.ShapeDtypeStruct((n, C_PAD), BF16)] * 2
        + [jax.ShapeDtypeStruct((n // seq, vt_rows, seq), BF16)],
        compiler_params=pltpu.CompilerParams(dimension_semantics=("arbitrary",),
                                             vmem_limit_bytes=VMEM_LIMIT),
        name="mla_proj",
    )(h, *params, *tables)


def _mla_attn_kernel(q_ref, k_ref, vt_ref, o_ref, sa_ref, sb_ref):
    i = pl.program_id(2)
    tq, tk = T_ATT, TK_ATT
    per_q = tq // tk
    qs = (q_ref[0, :, :LANES], q_ref[0, :, LANES:])

    def produce(chunk, buf, mask):
        off = pl.multiple_of(chunk * tk, tk)
        mx = []
        for hh in range(2):
            k = k_ref[0, pl.ds(off, tk), hh * LANES:(hh + 1) * LANES]
            s = lax.dot_general(k, qs[hh], _NT, preferred_element_type=F32)
            if mask is not None:
                s = jnp.where(mask, s, NEG)
            buf[hh] = s
            mx.append(jnp.max(s, axis=0, keepdims=True))
        return tuple(mx)

    def consume(chunk, buf, mx, state):
        off = pl.multiple_of(chunk * tk, tk)
        new = []
        for hh in range(2):
            m, acc = state[hh]
            vt = vt_ref[0, hh * C_V:(hh + 1) * C_V, pl.ds(off, tk)]
            m_new = jnp.maximum(m, mx[hh])
            p = jnp.exp2(buf[hh] - m_new).astype(BF16)
            pv = jnp.dot(jnp.concatenate([vt, ones], axis=0), p, preferred_element_type=F32)
            new.append((m_new, jnp.exp2(m - m_new) * acc + pv))
        return tuple(new)

    key = lax.broadcasted_iota(jnp.int32, (tk, tq), 0)
    qry = lax.broadcasted_iota(jnp.int32, (tk, tq), 1)
    d0 = i * per_q
    ones = jnp.ones((BF16_ROWS, tk), BF16)
    head0 = (jnp.full((1, tq), NEG, F32), jnp.zeros((C_V + BF16_ROWS, tq), F32))
    mx_a = produce(d0, sa_ref, key <= qry)
    mx_b = produce(d0 + 1, sb_ref, key + tk <= qry)
    state = consume(d0, sa_ref, mx_a, (head0, head0))

    def body(jj, carry):
        mx_b, state = carry
        pend = jnp.where(jj == 0, d0 + 1, 2 * jj - 1)
        mx_a = produce(2 * jj, sa_ref, None)
        state = consume(pend, sb_ref, mx_b, state)
        mx_b = produce(2 * jj + 1, sb_ref, None)
        state = consume(2 * jj, sa_ref, mx_a, state)
        return mx_b, state

    mx_b, state = lax.fori_loop(0, i, body, (mx_b, state))
    pend = jnp.where(i == 0, d0 + 1, 2 * i - 1)
    (_, a0), (_, a1) = consume(pend, sb_ref, mx_b, state)
    o = jnp.concatenate([a[:C_V] / a[C_V:C_V + 1] for a in (a0, a1)], axis=0)
    o_ref[0] = o.T.astype(o_ref.dtype)


def _mla_attn(q, k, vt):
    bn, s, _ = q.shape
    pair = 2 * LANES
    return pl.pallas_call(
        _mla_attn_kernel,
        grid=(bn, C_HEADS // 2, s // T_ATT),
        in_specs=[pl.BlockSpec((1, T_ATT, pair), lambda b, p, i: (b, i, p)),
                  pl.BlockSpec((1, s, pair), lambda b, p, i: (b, 0, p)),
                  pl.BlockSpec((1, 2 * C_V, s), lambda b, p, i: (b, p, 0))],
        out_specs=pl.BlockSpec((1, T_ATT, 2 * C_V), lambda b, p, i: (b, i, p)),
        out_shape=jax.ShapeDtypeStruct((bn, s, C_HEADS * C_V), BF16),
        scratch_shapes=[pltpu.VMEM((2, TK_ATT, T_ATT), F32)] * 2,
        compiler_params=pltpu.CompilerParams(
            dimension_semantics=("arbitrary", "arbitrary", "arbitrary"),
            vmem_limit_bytes=VMEM_LIMIT),
        name="mla_attn",
    )(q, k, vt)


def _even_colscale():
    cs = np.ones((1, AB_IN), np.float32)
    cs[:, :A_OUT] = HEAD_DIM ** -0.5
    for g in range(len(B_BRANCHES)):
        lo = A_IN + g * B_BRANCH_IN
        cs[:, lo:lo + B_OUT] = HEAD_DIM ** -0.5
    return jnp.asarray(cs)


def kernel(x, rel_bias, attn_norm, mlp_norm, final_norm, w_in_ab, sinks, w_out_ab, w_down_c, q_norm_c,
           w_uq_c, kv_norm_c, w_ukv_c, w_o_c, w_mlp_up, w_mlp_down):
    bn, s, d = x.shape
    depth = attn_norm.shape[0]
    n = bn * s
    bias_all = _bias_table(rel_bias)
    colscale = _even_colscale()
    tables = _rope_tables(s)
    row = lambda v: v.reshape(1, -1)
    h = x.reshape(n, d)
    for layer in range(depth):
        gf = row(final_norm) if layer == depth - 1 else None
        gm, wu, wd = row(mlp_norm[layer]), w_mlp_up[layer].astype(BF16), w_mlp_down[layer].astype(BF16)
        if layer % 2 == 0:
            e = layer // 2
            proj = _norm_proj(h, row(attn_norm[layer]), w_in_ab[e].astype(BF16), colscale)
            proj = proj.reshape(bn, s, AB_IN)
            oa = _band_a(proj, sinks[e], bias_all).reshape(n, A_OUT)
            obs = [_band_b(proj, bias_all, g, dil).reshape(n, 2 * B_OUT)
                   for g, (_, dil) in enumerate(B_BRANCHES)]
            h = _post(_post_even_kernel, "post_even", h, [oa] + obs, w_out_ab[e].astype(BF16),
                      gm, wu, wd, gf)
        else:
            o = layer // 2
            wd_ext, wq, wk, wvt = _mla_weights(w_down_c[o], w_uq_c[o], w_ukv_c[o])
            q, k, vt = _mla_proj(h, row(attn_norm[layer]), wd_ext, row(q_norm_c[o]), wq,
                                 row(kv_norm_c[o]), wk, wvt, tables, s)
            att = _mla_attn(q.reshape(bn, s, C_PAD), k.reshape(bn, s, C_PAD), vt)
            h = _post(_post_odd_kernel, "post_odd", h, [att.reshape(n, C_HEADS * C_V)],
                      w_o_c[o].astype(BF16), gm, wu, wd, gf)
    return h.reshape(bn, s, d)
```

```python
import functools
import math

import numpy as np
import jax
import jax.numpy as jnp
from jax import lax
from jax.experimental import pallas as pl
from jax.experimental.pallas import tpu as pltpu

F32 = jnp.float32
BF16 = jnp.bfloat16

D_MODEL = 1024
HEAD_DIM = 64
BLOCK = 128
EPS = 1e-6
NEG = -1e30
A_Q_HEADS = 8
A_KV_HEADS = 2
A_GROUP = A_Q_HEADS // A_KV_HEADS
A_WINDOW = 128
B_BRANCHES = ((128, 1), (512, 4), (2048, 16))
B_HEADS_PER_BRANCH = 4
NUM_BUCKETS = 32
MAX_DISTANCE = 2048
N_BIAS_HEADS = A_Q_HEADS + len(B_BRANCHES) * B_HEADS_PER_BRANCH
A_IN = (A_Q_HEADS + 2 * A_KV_HEADS) * HEAD_DIM
B_BRANCH_IN = 3 * B_HEADS_PER_BRANCH * HEAD_DIM
AB_IN = A_IN + len(B_BRANCHES) * B_BRANCH_IN
A_OUT = A_Q_HEADS * HEAD_DIM
B_OUT = B_HEADS_PER_BRANCH * HEAD_DIM
C_HEADS = 8
C_NOPE = 64
C_ROPE = 32
C_V = 64
C_Q_RANK = 384
C_KV_RANK = 256
ROPE_THETA = 10000.0
D_FF = 4 * D_MODEL

LANES = 128
BF16_ROWS = 16
C_PAD = C_HEADS * LANES
C_DOWN_EXT = C_Q_RANK + C_KV_RANK + LANES
ROPE_LO = C_NOPE
ROPE_HALF = C_ROPE // 2

VMEM_LIMIT = 52 * 1024 * 1024

TM_PROJ = 512
PERM_GROUP = 256
TQ_BAND = 512
TM_POST = 256
T_ATT = 512
TK_ATT = 256
assert T_ATT == 2 * TK_ATT

_NT = (((1,), (1,)), ((), ()))


def _const_spec(shape):
    nd = len(shape)
    return pl.BlockSpec(shape, lambda *_: (0,) * nd, pipeline_mode=pl.Buffered(1))


def _rms(x, g):
    ms = jnp.mean(x * x, axis=-1, keepdims=True)
    return x * lax.rsqrt(ms + EPS) * g


def _t5_bucket_np(n):
    max_exact = NUM_BUCKETS // 2
    nf = np.maximum(n, 1).astype(np.float32)
    large = max_exact + (np.log(nf / max_exact) / math.log(MAX_DISTANCE / max_exact)
                         * (NUM_BUCKETS - max_exact)).astype(np.int32)
    return np.where(n < max_exact, n, np.minimum(large, NUM_BUCKETS - 1))


def _band_bucket_index(dilation, max_dist):
    i = np.arange(BLOCK)[:, None]
    j = np.arange(2 * BLOCK)[None, :]
    dist = i + BLOCK - j
    inband = (dist >= 0) & (dist <= max_dist)
    return np.where(inband, _t5_bucket_np(np.maximum(dist, 0) * dilation), -1).astype(np.int32)


def _bias_kernel(tab_ref, idx_ref, o_ref):
    h = pl.program_id(0)
    idx = idx_ref[0]
    acc = jnp.full(idx.shape, NEG, F32)
    for b in range(NUM_BUCKETS):
        acc = jnp.where(idx == b, tab_ref[h * NUM_BUCKETS + b], acc)
    o_ref[0] = acc


def _bias_table(rel_bias):
    configs = [(1, A_WINDOW - 1)] + [(dil, window // dil) for window, dil in B_BRANCHES]
    idx = jnp.asarray(np.stack([_band_bucket_index(d, m) for d, m in configs]))
    tab = rel_bias.T.reshape(-1)

    def cfg(h):
        return jnp.where(h < A_Q_HEADS, 0, 1 + (h - A_Q_HEADS) // B_HEADS_PER_BRANCH)

    return pl.pallas_call(
        _bias_kernel,
        grid=(N_BIAS_HEADS,),
        in_specs=[pl.BlockSpec(memory_space=pltpu.SMEM),
                  pl.BlockSpec((1, BLOCK, 2 * BLOCK), lambda h: (cfg(h), 0, 0))],
        out_specs=pl.BlockSpec((1, BLOCK, 2 * BLOCK), lambda h: (h, 0, 0)),
        out_shape=jax.ShapeDtypeStruct((N_BIAS_HEADS, BLOCK, 2 * BLOCK), F32),
        name="bias_table",
    )(tab, idx)


def _deinterleave_matrix(dil):
    rho = np.arange(PERM_GROUP)
    per_res = PERM_GROUP // dil
    p = np.zeros((PERM_GROUP, PERM_GROUP), np.float32)
    p[rho, (rho % per_res) * dil + rho // per_res] = 1.0
    return jnp.asarray(p, BF16)


def _even_proj_kernel(x_ref, g_ref, w_ref, cs_ref, p1_ref, p2_ref, pa_ref, pb0_ref, pb1_ref, pb2_ref):
    xn = _rms(x_ref[...], g_ref[...]).astype(BF16)
    y = (jnp.dot(xn, w_ref[...], preferred_element_type=F32) * cs_ref[...]).astype(BF16)
    pa_ref[...] = y[:, :A_IN]
    pb0_ref[...] = y[:, A_IN:A_IN + B_BRANCH_IN]
    for g, (perm_ref, out_ref) in enumerate(((p1_ref, pb1_ref), (p2_ref, pb2_ref)), start=1):
        dil = B_BRANCHES[g][1]
        rows = PERM_GROUP // dil
        lo = A_IN + g * B_BRANCH_IN
        for grp in range(TM_PROJ // PERM_GROUP):
            yg = y[grp * PERM_GROUP:(grp + 1) * PERM_GROUP, lo:lo + B_BRANCH_IN]
            z = jnp.dot(perm_ref[...], yg, preferred_element_type=F32).astype(BF16)
            for r in range(dil):
                out_ref[0, r, grp * rows:(grp + 1) * rows, :] = z[r * rows:(r + 1) * rows]


def _even_proj(h, g, w, colscale, bn, seq):
    n, d = h.shape
    per_seq = seq // TM_PROJ
    tile = lambda w_: pl.BlockSpec((TM_PROJ, w_), lambda i: (i, 0))
    dils = [dil for _, dil in B_BRANCHES[1:]]
    dil_spec = lambda dil: pl.BlockSpec((1, dil, TM_PROJ // dil, B_BRANCH_IN),
                                        lambda i: (i // per_seq, 0, i % per_seq, 0))
    perms = [_deinterleave_matrix(dil) for dil in dils]
    return pl.pallas_call(
        _even_proj_kernel,
        grid=(n // TM_PROJ,),
        in_specs=[tile(d), _const_spec((1, d)), _const_spec(w.shape), _const_spec(colscale.shape)]
        + [_const_spec(p.shape) for p in perms],
        out_specs=[tile(A_IN), tile(B_BRANCH_IN)] + [dil_spec(dil) for dil in dils],
        out_shape=[jax.ShapeDtypeStruct((n, A_IN), BF16), jax.ShapeDtypeStruct((n, B_BRANCH_IN), BF16)]
        + [jax.ShapeDtypeStruct((bn, dil, seq // dil, B_BRANCH_IN), BF16) for dil in dils],
        compiler_params=pltpu.CompilerParams(dimension_semantics=("arbitrary",),
                                             vmem_limit_bytes=VMEM_LIMIT),
        name="even_proj",
    )(h, g, w, colscale, *perms)


def _band_head(q, k, v, bias, first_mask, sink):
    s = lax.dot_general(q, k, _NT, preferred_element_type=F32) + bias
    if first_mask is not None:
        s = jnp.where(first_mask, NEG, s)
    m = jnp.max(s, axis=-1, keepdims=True)
    if sink is not None:
        m = jnp.maximum(m, sink)
    p = jnp.exp(s - m)
    l = jnp.sum(p, axis=-1, keepdims=True)
    if sink is not None:
        l = l + jnp.exp(sink - m)
    o = jnp.dot(p.astype(BF16), v, preferred_element_type=F32)
    return o / l, m, l


def _first_block_mask(is_first_tile):
    col = lax.broadcasted_iota(jnp.int32, (BLOCK, 2 * BLOCK), 1)
    return jnp.logical_and(is_first_tile, col < BLOCK)


def _with_prev(cur, prev, lo, width):
    return jnp.concatenate([prev[:, lo:lo + width], cur[:, lo:lo + width]], axis=0)


def _band_a_kernel(sink_ref, cur_ref, prev_ref, bias_ref, o_ref):
    is_first = pl.program_id(1) == 0
    cur, prev = cur_ref[0], prev_ref[0]
    kd = A_KV_HEADS * HEAD_DIM
    k_all = _with_prev(cur, prev, A_OUT, kd)
    v_all = _with_prev(cur, prev, A_OUT + kd, kd)
    for j in range(TQ_BAND // BLOCK):
        rows = slice(j * BLOCK, (j + 1) * BLOCK)
        krows = slice(j * BLOCK, (j + 2) * BLOCK)
        mask = _first_block_mask(is_first) if j == 0 else None
        outs = []
        for h in range(A_Q_HEADS):
            kv = slice((h // A_GROUP) * HEAD_DIM, (h // A_GROUP + 1) * HEAD_DIM)
            o, _, _ = _band_head(cur[rows, h * HEAD_DIM:(h + 1) * HEAD_DIM], k_all[krows, kv],
                                 v_all[krows, kv], bias_ref[h], mask, sink_ref[h])
            outs.append(o)
        o_ref[0, rows, :] = jnp.concatenate(outs, axis=-1).astype(o_ref.dtype)


def _band_a(pa, sinks, bias_all):
    bn, s, _ = pa.shape
    per_tile = TQ_BAND // BLOCK
    return pl.pallas_call(
        _band_a_kernel,
        grid=(bn, s // TQ_BAND),
        in_specs=[pl.BlockSpec(memory_space=pltpu.SMEM),
                  pl.BlockSpec((1, TQ_BAND, A_IN), lambda b, n: (b, n, 0)),
                  pl.BlockSpec((1, BLOCK, A_IN), lambda b, n: (b, jnp.maximum(n * per_tile - 1, 0), 0)),
                  pl.BlockSpec((A_Q_HEADS, BLOCK, 2 * BLOCK), lambda b, n: (0, 0, 0))],
        out_specs=pl.BlockSpec((1, TQ_BAND, A_OUT), lambda b, n: (b, n, 0)),
        out_shape=jax.ShapeDtypeStruct((bn, s, A_OUT), BF16),
        compiler_params=pltpu.CompilerParams(dimension_semantics=("arbitrary", "arbitrary")),
        name="band_a",
    )(sinks, pa, pa, bias_all)


def _band_b_kernel(cur_ref, prev_ref, bias_ref, o_ref):
    is_first = pl.program_id(2) == 0
    cur, prev = cur_ref[0, 0], prev_ref[0, 0]
    k_all = _with_prev(cur, prev, B_OUT, B_OUT)
    v_all = _with_prev(cur, prev, 2 * B_OUT, B_OUT)
    for j in range(TQ_BAND // BLOCK):
        rows = slice(j * BLOCK, (j + 1) * BLOCK)
        krows = slice(j * BLOCK, (j + 2) * BLOCK)
        mask = _first_block_mask(is_first) if j == 0 else None
        outs, lses = [], []
        for h in range(B_HEADS_PER_BRANCH):
            sl = slice(h * HEAD_DIM, (h + 1) * HEAD_DIM)
            o, m, l = _band_head(cur[rows, sl], k_all[krows, sl], v_all[krows, sl], bias_ref[h], mask,
                                 None)
            outs.append(o)
            lses.append(jnp.broadcast_to(m + jnp.log(l), (BLOCK, HEAD_DIM)))
        o_ref[0, 0, rows, :] = jnp.concatenate(outs + lses, axis=-1)


def _band_b(pb, bias_all, g):
    bn, dil, l, _ = pb.shape
    per_tile = TQ_BAND // BLOCK
    bias_blk = (A_Q_HEADS + g * B_HEADS_PER_BRANCH) // B_HEADS_PER_BRANCH
    return pl.pallas_call(
        _band_b_kernel,
        grid=(bn, dil, l // TQ_BAND),
        in_specs=[pl.BlockSpec((1, 1, TQ_BAND, B_BRANCH_IN), lambda b, r, n: (b, r, n, 0)),
                  pl.BlockSpec((1, 1, BLOCK, B_BRANCH_IN),
                               lambda b, r, n: (b, r, jnp.maximum(n * per_tile - 1, 0), 0)),
                  pl.BlockSpec((B_HEADS_PER_BRANCH, BLOCK, 2 * BLOCK), lambda b, r, n: (bias_blk, 0, 0))],
        out_specs=pl.BlockSpec((1, 1, TQ_BAND, 2 * B_OUT), lambda b, r, n: (b, r, n, 0)),
        out_shape=jax.ShapeDtypeStruct((bn, dil, l, 2 * B_OUT), F32),
        compiler_params=pltpu.CompilerParams(
            dimension_semantics=("arbitrary", "arbitrary", "arbitrary")),
        name=f"band_b{g}",
    )(pb, pb, bias_all)


def _mlp_tail(h2, gm_ref, wu_ref, wd_ref, gf_ref, o_ref):
    xn = _rms(h2, gm_ref[...]).astype(BF16)
    u = jnp.maximum(jnp.dot(xn, wu_ref[...], preferred_element_type=F32), 0.0)
    u = (u * u).astype(BF16)
    h3 = h2 + jnp.dot(u, wd_ref[...], preferred_element_type=F32)
    if gf_ref is not None:
        h3 = _rms(h3, gf_ref[...])
    o_ref[...] = h3


def _token_major_slabs(ob_ref, scr_ref):
    dil = ob_ref.shape[1]
    rows = TM_POST // dil
    n_slabs = ob_ref.shape[3] // LANES
    for r in range(dil):
        blk = ob_ref[0, r]
        for c in range(n_slabs):
            scr_ref[c, pl.ds(r, rows, stride=dil), :] = blk[:, c * LANES:(c + 1) * LANES]
    return [scr_ref[c] for c in range(n_slabs)]


def _post_even_kernel(final, h_ref, oa_ref, b0_ref, b1_ref, b2_ref, wo_ref, gm_ref, wu_ref, wd_ref,
                      *rest):
    gf_ref = rest[0] if final else None
    o_ref, scr1_ref, scr2_ref = rest[-3:]
    b0 = b0_ref[...]
    slabs = [[b0[:, c * LANES:(c + 1) * LANES] for c in range(2 * B_OUT // LANES)],
             _token_major_slabs(b1_ref, scr1_ref), _token_major_slabs(b2_ref, scr2_ref)]
    n_o = B_OUT // LANES
    merged = []
    for c in range(n_o):
        lses = [sl[n_o + c] for sl in slabs]
        mx = jnp.maximum(jnp.maximum(lses[0], lses[1]), lses[2])
        es = [jnp.exp(ls - mx) for ls in lses]
        num = es[0] * slabs[0][c] + es[1] * slabs[1][c] + es[2] * slabs[2][c]
        merged.append(num / (es[0] + es[1] + es[2]))
    out_b = jnp.concatenate(merged, axis=-1).astype(BF16)
    mix = jnp.dot(oa_ref[...], wo_ref[:A_OUT, :], preferred_element_type=F32)
    mix = mix + jnp.dot(out_b, wo_ref[A_OUT:, :], preferred_element_type=F32)
    _mlp_tail(h_ref[...] + mix, gm_ref, wu_ref, wd_ref, gf_ref, o_ref)


def _post_odd_kernel(final, h_ref, a_ref, wo_ref, gm_ref, wu_ref, wd_ref, *rest):
    gf_ref, o_ref = (rest[0], rest[1]) if final else (None, rest[0])
    mix = jnp.dot(a_ref[...], wo_ref[...], preferred_element_type=F32)
    _mlp_tail(h_ref[...] + mix, gm_ref, wu_ref, wd_ref, gf_ref, o_ref)


def _post(kernel_fn, name, h, acts, act_specs, scratch, wo, gm, wu, wd, gf):
    n, d = h.shape
    final = gf is not None
    params = [wo, gm, wu, wd] + ([gf] if final else [])
    return pl.pallas_call(
        functools.partial(kernel_fn, final),
        grid=(n // TM_POST,),
        in_specs=[_row_tile(d)] + act_specs + [_const_spec(p.shape) for p in params],
        out_specs=_row_tile(d),
        out_shape=jax.ShapeDtypeStruct((n, d), F32),
        scratch_shapes=scratch,
        compiler_params=pltpu.CompilerParams(dimension_semantics=("arbitrary",),
                                             vmem_limit_bytes=VMEM_LIMIT),
        name=name,
    )(h, *acts, *params)


def _row_tile(width):
    return pl.BlockSpec((TM_POST, width), lambda i: (i, 0))


def _post_even(h, oa, ob0, ob1, ob2, seq, *params):
    per_seq = seq // TM_POST
    res_spec = lambda ob: pl.BlockSpec((1, ob.shape[1], TM_POST // ob.shape[1], ob.shape[3]),
                                       lambda i: (i // per_seq, 0, i % per_seq, 0))
    specs = [_row_tile(oa.shape[1]), _row_tile(ob0.shape[1]), res_spec(ob1), res_spec(ob2)]
    scratch = [pltpu.VMEM((2 * B_OUT // LANES, TM_POST, LANES), F32)] * 2
    return _post(_post_even_kernel, "post_even", h, [oa, ob0, ob1, ob2], specs, scratch, *params)


def _post_odd(h, att, *params):
    return _post(_post_odd_kernel, "post_odd", h, [att], [_row_tile(att.shape[1])], [], *params)


def _rope_lanes(x, cos, s1, s2):
    return x * cos + pltpu.roll(x, LANES - ROPE_HALF, 1) * s1 + pltpu.roll(x, ROPE_HALF, 1) * s2


def _mla_proj_kernel(scale, h_ref, g_ref, wd_ref, qn_ref, wq_ref, kn_ref, wk_ref, wv_ref,
                     cos_ref, s1_ref, s2_ref, q_ref, k_ref, vt_ref):
    xn = _rms(h_ref[...], g_ref[...]).astype(BF16)
    down = jnp.dot(xn, wd_ref[...], preferred_element_type=F32)
    cq = _rms(down[:, :C_Q_RANK], qn_ref[...]).astype(BF16)
    ckv = _rms(down[:, C_Q_RANK:C_Q_RANK + C_KV_RANK], kn_ref[...]).astype(BF16)
    cos, s1, s2 = cos_ref[...], s1_ref[...], s2_ref[...]
    kr = _rope_lanes(down[:, C_Q_RANK + C_KV_RANK:], cos, s1, s2)
    q = jnp.dot(cq, wq_ref[...], preferred_element_type=F32)
    k = jnp.dot(ckv, wk_ref[...], preferred_element_type=F32)
    lane = lax.broadcasted_iota(jnp.int32, (1, LANES), 1)
    cos_q = jnp.where(lane < C_NOPE, 1.0, cos)
    for hh in range(C_HEADS):
        sl = slice(hh * LANES, (hh + 1) * LANES)
        q_ref[:, sl] = (_rope_lanes(q[:, sl], cos_q, s1, s2) * scale).astype(q_ref.dtype)
        k_ref[:, sl] = (k[:, sl] + kr).astype(k_ref.dtype)
    vt_ref[0] = lax.dot_general(wv_ref[...], ckv, _NT, preferred_element_type=F32).astype(vt_ref.dtype)


def _rope_tables(s):
    inv = ROPE_THETA ** (-jnp.arange(0, C_ROPE, 2, dtype=F32) / C_ROPE)
    ang = jnp.arange(s, dtype=F32)[:, None] * inv[None, :]
    cos, sin = jnp.cos(ang), jnp.sin(ang)
    z = lambda w: jnp.zeros((s, w), F32)
    tail = LANES - ROPE_LO - C_ROPE
    cos_t = jnp.concatenate([z(ROPE_LO), cos, cos, z(tail)], axis=1)
    s1_t = jnp.concatenate([z(ROPE_LO), -sin, z(ROPE_HALF), z(tail)], axis=1)
    s2_t = jnp.concatenate([z(ROPE_LO), z(ROPE_HALF), sin, z(tail)], axis=1)
    return cos_t, s1_t, s2_t


def _pad_heads(w, width):
    r = w.shape[0]
    w3 = jnp.pad(w.reshape(r, C_HEADS, width), ((0, 0), (0, 0), (0, LANES - width)))
    return w3.reshape(r, C_PAD)


def _mla_weights(w_down, w_uq, w_ukv):
    kr = jnp.zeros((D_MODEL, LANES), w_down.dtype).at[:, ROPE_LO:ROPE_LO + C_ROPE].set(
        w_down[:, C_Q_RANK + C_KV_RANK:])
    wd_ext = jnp.concatenate([w_down[:, :C_Q_RANK + C_KV_RANK], kr], axis=1)
    wq = _pad_heads(w_uq, C_NOPE + C_ROPE)
    ukv = w_ukv.reshape(C_KV_RANK, C_HEADS, C_NOPE + C_V)
    wk = _pad_heads(ukv[:, :, :C_NOPE].reshape(C_KV_RANK, -1), C_NOPE)
    wvt = ukv[:, :, C_NOPE:].reshape(C_KV_RANK, -1).T
    return wd_ext.astype(BF16), wq.astype(BF16), wk.astype(BF16), wvt.astype(BF16)


def _mla_proj(h, g, wd_ext, qn, wq, kn, wk, wvt, tables, seq):
    n, d = h.shape
    scale = (C_NOPE + C_ROPE) ** -0.5 * math.log2(math.e)
    per_seq = seq // TM_PROJ
    tab_spec = pl.BlockSpec((TM_PROJ, LANES), lambda i: (i % per_seq, 0))
    tile = lambda w: pl.BlockSpec((TM_PROJ, w), lambda i: (i, 0))
    vt_rows = C_HEADS * C_V
    vt_spec = pl.BlockSpec((1, vt_rows, TM_PROJ), lambda i: (i // per_seq, 0, i % per_seq))
    params = [g, wd_ext, qn, wq, kn, wk, wvt]
    return pl.pallas_call(
        functools.partial(_mla_proj_kernel, scale),
        grid=(n // TM_PROJ,),
        in_specs=[tile(d)] + [_const_spec(p.shape) for p in params] + [tab_spec] * 3,
        out_specs=[tile(C_PAD), tile(C_PAD), vt_spec],
        out_shape=[jax.ShapeDtypeStruct((n, C_PAD), BF16)] * 2
        + [jax.ShapeDtypeStruct((n // seq, vt_rows, seq), BF16)],
        compiler_params=pltpu.CompilerParams(dimension_semantics=("arbitrary",),
                                             vmem_limit_bytes=VMEM_LIMIT),
        name="mla_proj",
    )(h, *params, *tables)


def _mla_attn_kernel(q_ref, k_ref, vt_ref, o_ref, sa_ref, sb_ref):
    i = pl.program_id(2)
    tq, tk = T_ATT, TK_ATT
    per_q = tq // tk
    qs = (q_ref[0, :, :LANES], q_ref[0, :, LANES:])

    def produce(chunk, buf, mask):
        off = pl.multiple_of(chunk * tk, tk)
        mx = []
        for hh in range(2):
            k = k_ref[0, pl.ds(off, tk), hh * LANES:(hh + 1) * LANES]
            s = lax.dot_general(k, qs[hh], _NT, preferred_element_type=F32)
            if mask is not None:
                s = jnp.where(mask, s, NEG)
            buf[hh] = s
            mx.append(jnp.max(s, axis=0, keepdims=True))
        return tuple(mx)

    def consume(chunk, buf, mx, state):
        off = pl.multiple_of(chunk * tk, tk)
        new = []
        for hh in range(2):
            m, acc = state[hh]
            vt = vt_ref[0, hh * C_V:(hh + 1) * C_V, pl.ds(off, tk)]
            m_new = jnp.maximum(m, mx[hh])
            p = jnp.exp2(buf[hh] - m_new).astype(BF16)
            pv = jnp.dot(jnp.concatenate([vt, ones], axis=0), p, preferred_element_type=F32)
            new.append((m_new, jnp.exp2(m - m_new) * acc + pv))
        return tuple(new)

    key = lax.broadcasted_iota(jnp.int32, (tk, tq), 0)
    qry = lax.broadcasted_iota(jnp.int32, (tk, tq), 1)
    d0 = i * per_q
    ones = jnp.ones((BF16_ROWS, tk), BF16)
    head0 = (jnp.full((1, tq), NEG, F32), jnp.zeros((C_V + BF16_ROWS, tq), F32))
    mx_a = produce(d0, sa_ref, key <= qry)
    mx_b = produce(d0 + 1, sb_ref, key + tk <= qry)
    state = consume(d0, sa_ref, mx_a, (head0, head0))

    def body(jj, carry):
        mx_b, state = carry
        pend = jnp.where(jj == 0, d0 + 1, 2 * jj - 1)
        mx_a = produce(2 * jj, sa_ref, None)
        state = consume(pend, sb_ref, mx_b, state)
        mx_b = produce(2 * jj + 1, sb_ref, None)
        state = consume(2 * jj, sa_ref, mx_a, state)
        return mx_b, state

    mx_b, state = lax.fori_loop(0, i, body, (mx_b, state))
    pend = jnp.where(i == 0, d0 + 1, 2 * i - 1)
    (_, a0), (_, a1) = consume(pend, sb_ref, mx_b, state)
    o = jnp.concatenate([a[:C_V] / a[C_V:C_V + 1] for a in (a0, a1)], axis=0)
    o_ref[0] = o.T.astype(o_ref.dtype)


def _mla_attn(q, k, vt):
    bn, s, _ = q.shape
    pair = 2 * LANES
    return pl.pallas_call(
        _mla_attn_kernel,
        grid=(bn, C_HEADS // 2, s // T_ATT),
        in_specs=[pl.BlockSpec((1, T_ATT, pair), lambda b, p, i: (b, i, p)),
                  pl.BlockSpec((1, s, pair), lambda b, p, i: (b, 0, p)),
                  pl.BlockSpec((1, 2 * C_V, s), lambda b, p, i: (b, p, 0))],
        out_specs=pl.BlockSpec((1, T_ATT, 2 * C_V), lambda b, p, i: (b, i, p)),
        out_shape=jax.ShapeDtypeStruct((bn, s, C_HEADS * C_V), BF16),
        scratch_shapes=[pltpu.VMEM((2, TK_ATT, T_ATT), F32)] * 2,
        compiler_params=pltpu.CompilerParams(
            dimension_semantics=("arbitrary", "arbitrary", "arbitrary"),
            vmem_limit_bytes=VMEM_LIMIT),
        name="mla_attn",
    )(q, k, vt)


def _even_colscale():
    cs = np.ones((1, AB_IN), np.float32)
    cs[:, :A_OUT] = HEAD_DIM ** -0.5
    for g in range(len(B_BRANCHES)):
        lo = A_IN + g * B_BRANCH_IN
        cs[:, lo:lo + B_OUT] = HEAD_DIM ** -0.5
    return jnp.asarray(cs)


def kernel(x, rel_bias, attn_norm, mlp_norm, final_norm, w_in_ab, sinks, w_out_ab, w_down_c, q_norm_c,
           w_uq_c, kv_norm_c, w_ukv_c, w_o_c, w_mlp_up, w_mlp_down):
    bn, s, d = x.shape
    depth = attn_norm.shape[0]
    n = bn * s
    bias_all = _bias_table(rel_bias)
    colscale = _even_colscale()
    tables = _rope_tables(s)
    row = lambda v: v.reshape(1, -1)
    h = x.reshape(n, d)
    for layer in range(depth):
        gf = row(final_norm) if layer == depth - 1 else None
        mlp = (row(mlp_norm[layer]), w_mlp_up[layer].astype(BF16), w_mlp_down[layer].astype(BF16), gf)
        if layer % 2 == 0:
            e = layer // 2
            pa, pb0, pb1, pb2 = _even_proj(h, row(attn_norm[layer]), w_in_ab[e].astype(BF16), colscale,
                                           bn, s)
            oa = _band_a(pa.reshape(bn, s, A_IN), sinks[e], bias_all)
            ob0 = _band_b(pb0.reshape(bn, 1, s, B_BRANCH_IN), bias_all, 0)
            ob1 = _band_b(pb1, bias_all, 1)
            ob2 = _band_b(pb2, bias_all, 2)
            h = _post_even(h, oa.reshape(n, A_OUT), ob0.reshape(n, 2 * B_OUT), ob1, ob2, s,
                           w_out_ab[e].astype(BF16), *mlp)
        else:
            o = layer // 2
            wd_ext, wq, wk, wvt = _mla_weights(w_down_c[o], w_uq_c[o], w_ukv_c[o])
            q, k, vt = _mla_proj(h, row(attn_norm[layer]), wd_ext, row(q_norm_c[o]), wq,
                                 row(kv_norm_c[o]), wk, wvt, tables, s)
            att = _mla_attn(q.reshape(bn, s, C_PAD), k.reshape(bn, s, C_PAD), vt)
            h = _post_odd(h, att.reshape(n, C_HEADS * C_V), w_o_c[o].astype(BF16), *mlp)
    return h.reshape(bn, s, d)
```

```python
import functools
import math

import numpy as np
import jax
import jax.numpy as jnp
from jax import lax
from jax.experimental import pallas as pl
from jax.experimental.pallas import tpu as pltpu

F32 = jnp.float32
BF16 = jnp.bfloat16

D_MODEL = 1024
HEAD_DIM = 64
BLOCK = 128
EPS = 1e-6
NEG = -1e30
A_Q_HEADS = 8
A_KV_HEADS = 2
A_GROUP = A_Q_HEADS // A_KV_HEADS
A_WINDOW = 128
B_BRANCHES = ((128, 1), (512, 4), (2048, 16))
B_HEADS_PER_BRANCH = 4
NUM_BUCKETS = 32
MAX_DISTANCE = 2048
N_BIAS_HEADS = A_Q_HEADS + len(B_BRANCHES) * B_HEADS_PER_BRANCH
A_IN = (A_Q_HEADS + 2 * A_KV_HEADS) * HEAD_DIM
B_BRANCH_IN = 3 * B_HEADS_PER_BRANCH * HEAD_DIM
AB_IN = A_IN + len(B_BRANCHES) * B_BRANCH_IN
A_OUT = A_Q_HEADS * HEAD_DIM
B_OUT = B_HEADS_PER_BRANCH * HEAD_DIM
C_HEADS = 8
C_NOPE = 64
C_ROPE = 32
C_V = 64
C_Q_RANK = 384
C_KV_RANK = 256
ROPE_THETA = 10000.0
D_FF = 4 * D_MODEL

LANES = 128
BF16_ROWS = 16
C_PAD = C_HEADS * LANES
C_DOWN_EXT = C_Q_RANK + C_KV_RANK + LANES
ROPE_LO = C_NOPE
ROPE_HALF = C_ROPE // 2

VMEM_LIMIT = 52 * 1024 * 1024

TM_PROJ = 512
PERM_GROUP = 256
TQ_BAND = 512
TM_POST = 256
T_ATT = 512
TK_ATT = 256
assert T_ATT == 2 * TK_ATT

_NT = (((1,), (1,)), ((), ()))


def _const_spec(shape):
    nd = len(shape)
    return pl.BlockSpec(shape, lambda *_: (0,) * nd, pipeline_mode=pl.Buffered(1))


def _rms(x, g):
    ms = jnp.mean(x * x, axis=-1, keepdims=True)
    return x * lax.rsqrt(ms + EPS) * g


def _t5_bucket_np(n):
    max_exact = NUM_BUCKETS // 2
    nf = np.maximum(n, 1).astype(np.float32)
    large = max_exact + (np.log(nf / max_exact) / math.log(MAX_DISTANCE / max_exact)
                         * (NUM_BUCKETS - max_exact)).astype(np.int32)
    return np.where(n < max_exact, n, np.minimum(large, NUM_BUCKETS - 1))


def _band_bucket_index(dilation, max_dist):
    i = np.arange(BLOCK)[:, None]
    j = np.arange(2 * BLOCK)[None, :]
    dist = i + BLOCK - j
    inband = (dist >= 0) & (dist <= max_dist)
    return np.where(inband, _t5_bucket_np(np.maximum(dist, 0) * dilation), -1).astype(np.int32)


def _bias_kernel(tab_ref, idx_ref, o_ref):
    h = pl.program_id(0)
    idx = idx_ref[0]
    acc = jnp.full(idx.shape, NEG, F32)
    for b in range(NUM_BUCKETS):
        acc = jnp.where(idx == b, tab_ref[h * NUM_BUCKETS + b], acc)
    o_ref[0] = acc


def _bias_table(rel_bias):
    configs = [(1, A_WINDOW - 1)] + [(dil, window // dil) for window, dil in B_BRANCHES]
    idx = jnp.asarray(np.stack([_band_bucket_index(d, m) for d, m in configs]))
    tab = rel_bias.T.reshape(-1)

    def cfg(h):
        return jnp.where(h < A_Q_HEADS, 0, 1 + (h - A_Q_HEADS) // B_HEADS_PER_BRANCH)

    return pl.pallas_call(
        _bias_kernel,
        grid=(N_BIAS_HEADS,),
        in_specs=[pl.BlockSpec(memory_space=pltpu.SMEM),
                  pl.BlockSpec((1, BLOCK, 2 * BLOCK), lambda h: (cfg(h), 0, 0))],
        out_specs=pl.BlockSpec((1, BLOCK, 2 * BLOCK), lambda h: (h, 0, 0)),
        out_shape=jax.ShapeDtypeStruct((N_BIAS_HEADS, BLOCK, 2 * BLOCK), F32),
        name="bias_table",
    )(tab, idx)


def _deinterleave_matrix(dil):
    rho = np.arange(PERM_GROUP)
    per_res = PERM_GROUP // dil
    p = np.zeros((PERM_GROUP, PERM_GROUP), np.float32)
    p[rho, (rho % per_res) * dil + rho // per_res] = 1.0
    return jnp.asarray(p, BF16)


def _even_proj_kernel(x_ref, g_ref, w_ref, cs_ref, p1_ref, p2_ref, pa_ref, pb0_ref, pb1_ref, pb2_ref):
    xn = _rms(x_ref[...], g_ref[...]).astype(BF16)
    y = (jnp.dot(xn, w_ref[...], preferred_element_type=F32) * cs_ref[...]).astype(BF16)
    pa_ref[...] = y[:, :A_IN]
    pb0_ref[...] = y[:, A_IN:A_IN + B_BRANCH_IN]
    for g, (perm_ref, out_ref) in enumerate(((p1_ref, pb1_ref), (p2_ref, pb2_ref)), start=1):
        dil = B_BRANCHES[g][1]
        rows = PERM_GROUP // dil
        lo = A_IN + g * B_BRANCH_IN
        for grp in range(TM_PROJ // PERM_GROUP):
            yg = y[grp * PERM_GROUP:(grp + 1) * PERM_GROUP, lo:lo + B_BRANCH_IN]
            z = jnp.dot(perm_ref[...], yg, preferred_element_type=F32).astype(BF16)
            for r in range(dil):
                out_ref[0, r, grp * rows:(grp + 1) * rows, :] = z[r * rows:(r + 1) * rows]


def _even_proj(h, g, w, colscale, bn, seq):
    n, d = h.shape
    per_seq = seq // TM_PROJ
    tile = lambda w_: pl.BlockSpec((TM_PROJ, w_), lambda i: (i, 0))
    dils = [dil for _, dil in B_BRANCHES[1:]]
    dil_spec = lambda dil: pl.BlockSpec((1, dil, TM_PROJ // dil, B_BRANCH_IN),
                                        lambda i: (i // per_seq, 0, i % per_seq, 0))
    perms = [_deinterleave_matrix(dil) for dil in dils]
    return pl.pallas_call(
        _even_proj_kernel,
        grid=(n // TM_PROJ,),
        in_specs=[tile(d), _const_spec((1, d)), _const_spec(w.shape), _const_spec(colscale.shape)]
        + [_const_spec(p.shape) for p in perms],
        out_specs=[tile(A_IN), tile(B_BRANCH_IN)] + [dil_spec(dil) for dil in dils],
        out_shape=[jax.ShapeDtypeStruct((n, A_IN), BF16), jax.ShapeDtypeStruct((n, B_BRANCH_IN), BF16)]
        + [jax.ShapeDtypeStruct((bn, dil, seq // dil, B_BRANCH_IN), BF16) for dil in dils],
        compiler_params=pltpu.CompilerParams(dimension_semantics=("arbitrary",),
                                             vmem_limit_bytes=VMEM_LIMIT),
        name="even_proj",
    )(h, g, w, colscale, *perms)


def _band_head(q, k, v, bias, first_mask, sink):
    s = lax.dot_general(q, k, _NT, preferred_element_type=F32) + bias
    if first_mask is not None:
        s = jnp.where(first_mask, NEG, s)
    m = jnp.max(s, axis=-1, keepdims=True)
    if sink is not None:
        m = jnp.maximum(m, sink)
    p = jnp.exp(s - m)
    l = jnp.sum(p, axis=-1, keepdims=True)
    if sink is not None:
        l = l + jnp.exp(sink - m)
    o = jnp.dot(p.astype(BF16), v, preferred_element_type=F32)
    return o / l, m, l


def _first_block_mask(is_first_tile):
    col = lax.broadcasted_iota(jnp.int32, (BLOCK, 2 * BLOCK), 1)
    return jnp.logical_and(is_first_tile, col < BLOCK)


def _with_prev(cur, prev, lo, width):
    return jnp.concatenate([prev[:, lo:lo + width], cur[:, lo:lo + width]], axis=0)


def _band_a_kernel(sink_ref, cur_ref, prev_ref, bias_ref, o_ref):
    is_first = pl.program_id(1) == 0
    cur, prev = cur_ref[0], prev_ref[0]
    kd = A_KV_HEADS * HEAD_DIM
    k_all = _with_prev(cur, prev, A_OUT, kd)
    v_all = _with_prev(cur, prev, A_OUT + kd, kd)
    for j in range(TQ_BAND // BLOCK):
        rows = slice(j * BLOCK, (j + 1) * BLOCK)
        krows = slice(j * BLOCK, (j + 2) * BLOCK)
        mask = _first_block_mask(is_first) if j == 0 else None
        outs = []
        for h in range(A_Q_HEADS):
            kv = slice((h // A_GROUP) * HEAD_DIM, (h // A_GROUP + 1) * HEAD_DIM)
            o, _, _ = _band_head(cur[rows, h * HEAD_DIM:(h + 1) * HEAD_DIM], k_all[krows, kv],
                                 v_all[krows, kv], bias_ref[h], mask, sink_ref[h])
            outs.append(o)
        o_ref[0, rows, :] = jnp.concatenate(outs, axis=-1).astype(o_ref.dtype)


def _band_a(pa, sinks, bias_all):
    bn, s, _ = pa.shape
    per_tile = TQ_BAND // BLOCK
    return pl.pallas_call(
        _band_a_kernel,
        grid=(bn, s // TQ_BAND),
        in_specs=[pl.BlockSpec(memory_space=pltpu.SMEM),
                  pl.BlockSpec((1, TQ_BAND, A_IN), lambda b, n: (b, n, 0)),
                  pl.BlockSpec((1, BLOCK, A_IN), lambda b, n: (b, jnp.maximum(n * per_tile - 1, 0), 0)),
                  pl.BlockSpec((A_Q_HEADS, BLOCK, 2 * BLOCK), lambda b, n: (0, 0, 0))],
        out_specs=pl.BlockSpec((1, TQ_BAND, A_OUT), lambda b, n: (b, n, 0)),
        out_shape=jax.ShapeDtypeStruct((bn, s, A_OUT), BF16),
        compiler_params=pltpu.CompilerParams(dimension_semantics=("arbitrary", "arbitrary")),
        name="band_a",
    )(sinks, pa, pa, bias_all)


def _band_b_kernel(cur_ref, prev_ref, bias_ref, o_ref):
    is_first = pl.program_id(2) == 0
    cur, prev = cur_ref[0, 0], prev_ref[0, 0]
    k_all = _with_prev(cur, prev, B_OUT, B_OUT)
    v_all = _with_prev(cur, prev, 2 * B_OUT, B_OUT)
    for j in range(TQ_BAND // BLOCK):
        rows = slice(j * BLOCK, (j + 1) * BLOCK)
        krows = slice(j * BLOCK, (j + 2) * BLOCK)
        mask = _first_block_mask(is_first) if j == 0 else None
        outs, lses = [], []
        for h in range(B_HEADS_PER_BRANCH):
            sl = slice(h * HEAD_DIM, (h + 1) * HEAD_DIM)
            o, m, l = _band_head(cur[rows, sl], k_all[krows, sl], v_all[krows, sl], bias_ref[h], mask,
                                 None)
            outs.append(o)
            lses.append(jnp.broadcast_to(m + jnp.log(l), (BLOCK, HEAD_DIM)))
        o_ref[0, 0, rows, :] = jnp.concatenate(outs + lses, axis=-1)


def _band_b(pb, bias_all, g):
    bn, dil, l, _ = pb.shape
    per_tile = TQ_BAND // BLOCK
    bias_blk = (A_Q_HEADS + g * B_HEADS_PER_BRANCH) // B_HEADS_PER_BRANCH
    return pl.pallas_call(
        _band_b_kernel,
        grid=(bn, dil, l // TQ_BAND),
        in_specs=[pl.BlockSpec((1, 1, TQ_BAND, B_BRANCH_IN), lambda b, r, n: (b, r, n, 0)),
                  pl.BlockSpec((1, 1, BLOCK, B_BRANCH_IN),
                               lambda b, r, n: (b, r, jnp.maximum(n * per_tile - 1, 0), 0)),
                  pl.BlockSpec((B_HEADS_PER_BRANCH, BLOCK, 2 * BLOCK), lambda b, r, n: (bias_blk, 0, 0))],
        out_specs=pl.BlockSpec((1, 1, TQ_BAND, 2 * B_OUT), lambda b, r, n: (b, r, n, 0)),
        out_shape=jax.ShapeDtypeStruct((bn, dil, l, 2 * B_OUT), F32),
        compiler_params=pltpu.CompilerParams(
            dimension_semantics=("arbitrary", "arbitrary", "arbitrary")),
        name=f"band_b{g}",
    )(pb, pb, bias_all)


def _mlp_tail(h2, gm_ref, wu_ref, wd_ref, gf_ref, o_ref):
    xn = _rms(h2, gm_ref[...]).astype(BF16)
    u = jnp.maximum(jnp.dot(xn, wu_ref[...], preferred_element_type=F32), 0.0)
    u = (u * u).astype(BF16)
    h3 = h2 + jnp.dot(u, wd_ref[...], preferred_element_type=F32)
    if gf_ref is not None:
        h3 = _rms(h3, gf_ref[...])
    o_ref[...] = h3


def _token_major_slabs(ob_ref, scr_ref):
    dil = ob_ref.shape[1]
    rows = TM_POST // dil
    n_slabs = ob_ref.shape[3] // LANES
    for r in range(dil):
        blk = ob_ref[0, r]
        for c in range(n_slabs):
            scr_ref[c, pl.ds(r, rows, stride=dil), :] = blk[:, c * LANES:(c + 1) * LANES]
    return [scr_ref[c] for c in range(n_slabs)]


def _post_even_kernel(final, h_ref, oa_ref, b0_ref, b1_ref, b2_ref, wo_ref, gm_ref, wu_ref, wd_ref,
                      *rest):
    gf_ref = rest[0] if final else None
    o_ref, scr1_ref, scr2_ref = rest[-3:]
    b0 = b0_ref[...]
    slabs = [[b0[:, c * LANES:(c + 1) * LANES] for c in range(2 * B_OUT // LANES)],
             _token_major_slabs(b1_ref, scr1_ref), _token_major_slabs(b2_ref, scr2_ref)]
    n_o = B_OUT // LANES
    merged = []
    for c in range(n_o):
        lses = [sl[n_o + c] for sl in slabs]
        mx = jnp.maximum(jnp.maximum(lses[0], lses[1]), lses[2])
        es = [jnp.exp(ls - mx) for ls in lses]
        num = es[0] * slabs[0][c] + es[1] * slabs[1][c] + es[2] * slabs[2][c]
        merged.append(num / (es[0] + es[1] + es[2]))
    out_b = jnp.concatenate(merged, axis=-1).astype(BF16)
    mix = jnp.dot(oa_ref[...], wo_ref[:A_OUT, :], preferred_element_type=F32)
    mix = mix + jnp.dot(out_b, wo_ref[A_OUT:, :], preferred_element_type=F32)
    _mlp_tail(h_ref[...] + mix, gm_ref, wu_ref, wd_ref, gf_ref, o_ref)


def _post_odd_kernel(final, h_ref, a_ref, wo_ref, gm_ref, wu_ref, wd_ref, *rest):
    gf_ref, o_ref = (rest[0], rest[1]) if final else (None, rest[0])
    mix = jnp.dot(a_ref[...], wo_ref[...], preferred_element_type=F32)
    _mlp_tail(h_ref[...] + mix, gm_ref, wu_ref, wd_ref, gf_ref, o_ref)


def _post(kernel_fn, name, h, acts, act_specs, scratch, wo, gm, wu, wd, gf):
    n, d = h.shape
    final = gf is not None
    params = [wo, gm, wu, wd] + ([gf] if final else [])
    return pl.pallas_call(
        functools.partial(kernel_fn, final),
        grid=(n // TM_POST,),
        in_specs=[_row_tile(d)] + act_specs + [_const_spec(p.shape) for p in params],
        out_specs=_row_tile(d),
        out_shape=jax.ShapeDtypeStruct((n, d), F32),
        scratch_shapes=scratch,
        compiler_params=pltpu.CompilerParams(dimension_semantics=("arbitrary",),
                                             vmem_limit_bytes=VMEM_LIMIT),
        name=name,
    )(h, *acts, *params)


def _row_tile(width):
    return pl.BlockSpec((TM_POST, width), lambda i: (i, 0))


def _post_even(h, oa, ob0, ob1, ob2, seq, *params):
    per_seq = seq // TM_POST
    res_spec = lambda ob: pl.BlockSpec((1, ob.shape[1], TM_POST // ob.shape[1], ob.shape[3]),
                                       lambda i: (i // per_seq, 0, i % per_seq, 0))
    specs = [_row_tile(oa.shape[1]), _row_tile(ob0.shape[1]), res_spec(ob1), res_spec(ob2)]
    scratch = [pltpu.VMEM((2 * B_OUT // LANES, TM_POST, LANES), F32)] * 2
    return _post(_post_even_kernel, "post_even", h, [oa, ob0, ob1, ob2], specs, scratch, *params)


def _post_odd(h, att, *params):
    return _post(_post_odd_kernel, "post_odd", h, [att], [_row_tile(att.shape[1])], [], *params)


def _rope_lanes(x, cos, s1, s2):
    return x * cos + pltpu.roll(x, LANES - ROPE_HALF, 1) * s1 + pltpu.roll(x, ROPE_HALF, 1) * s2


def _mla_proj_kernel(scale, h_ref, g_ref, wd_ref, qn_ref, wq_ref, kn_ref, wk_ref, wv_ref,
                     cos_ref, s1_ref, s2_ref, q_ref, k_ref, vt_ref):
    xn = _rms(h_ref[...], g_ref[...]).astype(BF16)
    down = jnp.dot(xn, wd_ref[...], preferred_element_type=F32)
    cq = _rms(down[:, :C_Q_RANK], qn_ref[...]).astype(BF16)
    ckv = _rms(down[:, C_Q_RANK:C_Q_RANK + C_KV_RANK], kn_ref[...]).astype(BF16)
    cos, s1, s2 = cos_ref[...], s1_ref[...], s2_ref[...]
    kr = _rope_lanes(down[:, C_Q_RANK + C_KV_RANK:], cos, s1, s2)
    q = jnp.dot(cq, wq_ref[...], preferred_element_type=F32)
    k = jnp.dot(ckv, wk_ref[...], preferred_element_type=F32)
    lane = lax.broadcasted_iota(jnp.int32, (1, LANES), 1)
    cos_q = jnp.where(lane < C_NOPE, 1.0, cos)
    for hh in range(C_HEADS):
        sl = slice(hh * LANES, (hh + 1) * LANES)
        q_ref[:, sl] = (_rope_lanes(q[:, sl], cos_q, s1, s2) * scale).astype(q_ref.dtype)
        k_ref[:, sl] = (k[:, sl] + kr).astype(k_ref.dtype)
    vt_ref[0] = lax.dot_general(wv_ref[...], ckv, _NT, preferred_element_type=F32).astype(vt_ref.dtype)


def _rope_tables(s):
    inv = ROPE_THETA ** (-jnp.arange(0, C_ROPE, 2, dtype=F32) / C_ROPE)
    ang = jnp.arange(s, dtype=F32)[:, None] * inv[None, :]
    cos, sin = jnp.cos(ang), jnp.sin(ang)
    z = lambda w: jnp.zeros((s, w), F32)
    tail = LANES - ROPE_LO - C_ROPE
    cos_t = jnp.concatenate([z(ROPE_LO), cos, cos, z(tail)], axis=1)
    s1_t = jnp.concatenate([z(ROPE_LO), -sin, z(ROPE_HALF), z(tail)], axis=1)
    s2_t = jnp.concatenate([z(ROPE_LO), z(ROPE_HALF), sin, z(tail)], axis=1)
    return cos_t, s1_t, s2_t


def _pad_heads(w, width):
    r = w.shape[0]
    w3 = jnp.pad(w.reshape(r, C_HEADS, width), ((0, 0), (0, 0), (0, LANES - width)))
    return w3.reshape(r, C_PAD)


def _mla_weights(w_down, w_uq, w_ukv):
    kr = jnp.zeros((D_MODEL, LANES), w_down.dtype).at[:, ROPE_LO:ROPE_LO + C_ROPE].set(
        w_down[:, C_Q_RANK + C_KV_RANK:])
    wd_ext = jnp.concatenate([w_down[:, :C_Q_RANK + C_KV_RANK], kr], axis=1)
    wq = _pad_heads(w_uq, C_NOPE + C_ROPE)
    ukv = w_ukv.reshape(C_KV_RANK, C_HEADS, C_NOPE + C_V)
    wk = _pad_heads(ukv[:, :, :C_NOPE].reshape(C_KV_RANK, -1), C_NOPE)
    wvt = ukv[:, :, C_NOPE:].reshape(C_KV_RANK, -1).T
    return wd_ext.astype(BF16), wq.astype(BF16), wk.astype(BF16), wvt.astype(BF16)


def _mla_proj(h, g, wd_ext, qn, wq, kn, wk, wvt, tables, seq):
    n, d = h.shape
    scale = (C_NOPE + C_ROPE) ** -0.5 * math.log2(math.e)
    per_seq = seq // TM_PROJ
    tab_spec = pl.BlockSpec((TM_PROJ, LANES), lambda i: (i % per_seq, 0))
    tile = lambda w: pl.BlockSpec((TM_PROJ, w), lambda i: (i, 0))
    vt_rows = C_HEADS * C_V
    vt_spec = pl.BlockSpec((1, vt_rows, TM_PROJ), lambda i: (i // per_seq, 0, i % per_seq))
    params = [g, wd_ext, qn, wq, kn, wk, wvt]
    return pl.pallas_call(
        functools.partial(_mla_proj_kernel, scale),
        grid=(n // TM_PROJ,),
        in_specs=[tile(d)] + [_const_spec(p.shape) for p in params] + [tab_spec] * 3,
        out_specs=[tile(C_PAD), tile(C_PAD), vt_spec],
        out_shape=[jax.ShapeDtypeStruct((n, C_PAD), BF16)] * 2
        + [jax.ShapeDtypeStruct((n // seq, vt_rows, seq), BF16)],
        compiler_params=pltpu.CompilerParams(dimension_semantics=("arbitrary",),
                                             vmem_limit_bytes=VMEM_LIMIT),
        name="mla_proj",
    )(h, *params, *tables)


def _mla_attn_kernel(q_ref, k_ref, vt_ref, o_ref, sa_ref, sb_ref):
    i = pl.program_id(2)
    tq, tk = T_ATT, TK_ATT
    per_q = tq // tk
    qt = q_ref[0].T
    qs = (qt[:LANES], qt[LANES:])

    def produce(chunk, buf, mask):
        off = pl.multiple_of(chunk * tk, tk)
        mx = []
        for hh in range(2):
            k = k_ref[0, pl.ds(off, tk), hh * LANES:(hh + 1) * LANES]
            s = jnp.dot(k, qs[hh], preferred_element_type=F32)
            if mask is not None:
                s = jnp.where(mask, s, NEG)
            buf[hh] = s
            mx.append(jnp.max(s, axis=0, keepdims=True))
        return tuple(mx)

    def consume(chunk, buf, mx, state):
        off = pl.multiple_of(chunk * tk, tk)
        new = []
        for hh in range(2):
            m, acc = state[hh]
            vt = vt_ref[0, hh * C_V:(hh + 1) * C_V, pl.ds(off, tk)]
            m_new = jnp.maximum(m, mx[hh])
            p = jnp.exp2(buf[hh] - m_new).astype(BF16)
            pv = jnp.dot(jnp.concatenate([vt, ones], axis=0), p, preferred_element_type=F32)
            new.append((m_new, jnp.exp2(m - m_new) * acc + pv))
        return tuple(new)

    key = lax.broadcasted_iota(jnp.int32, (tk, tq), 0)
    qry = lax.broadcasted_iota(jnp.int32, (tk, tq), 1)
    d0 = i * per_q
    ones = jnp.ones((BF16_ROWS, tk), BF16)
    head0 = (jnp.full((1, tq), NEG, F32), jnp.zeros((C_V + BF16_ROWS, tq), F32))
    mx_a = produce(d0, sa_ref, key <= qry)
    mx_b = produce(d0 + 1, sb_ref, key + tk <= qry)
    state = consume(d0, sa_ref, mx_a, (head0, head0))

    def body(jj, carry):
        mx_b, state = carry
        pend = jnp.where(jj == 0, d0 + 1, 2 * jj - 1)
        mx_a = produce(2 * jj, sa_ref, None)
        state = consume(pend, sb_ref, mx_b, state)
        mx_b = produce(2 * jj + 1, sb_ref, None)
        state = consume(2 * jj, sa_ref, mx_a, state)
        return mx_b, state

    mx_b, state = lax.fori_loop(0, i, body, (mx_b, state))
    pend = jnp.where(i == 0, d0 + 1, 2 * i - 1)
    (_, a0), (_, a1) = consume(pend, sb_ref, mx_b, state)
    o = jnp.concatenate([a[:C_V] / a[C_V:C_V + 1] for a in (a0, a1)], axis=0)
    o_ref[0] = o.T.astype(o_ref.dtype)


def _mla_attn(q, k, vt):
    bn, s, _ = q.shape
    pair = 2 * LANES
    return pl.pallas_call(
        _mla_attn_kernel,
        grid=(bn, C_HEADS // 2, s // T_ATT),
        in_specs=[pl.BlockSpec((1, T_ATT, pair), lambda b, p, i: (b, i, p)),
                  pl.BlockSpec((1, s, pair), lambda b, p, i: (b, 0, p)),
                  pl.BlockSpec((1, 2 * C_V, s), lambda b, p, i: (b, p, 0))],
        out_specs=pl.BlockSpec((1, T_ATT, 2 * C_V), lambda b, p, i: (b, i, p)),
        out_shape=jax.ShapeDtypeStruct((bn, s, C_HEADS * C_V), BF16),
        scratch_shapes=[pltpu.VMEM((2, TK_ATT, T_ATT), F32)] * 2,
        compiler_params=pltpu.CompilerParams(
            dimension_semantics=("arbitrary", "arbitrary", "arbitrary"),
            vmem_limit_bytes=VMEM_LIMIT),
        name="mla_attn",
    )(q, k, vt)


def _even_colscale():
    cs = np.ones((1, AB_IN), np.float32)
    cs[:, :A_OUT] = HEAD_DIM ** -0.5
    for g in range(len(B_BRANCHES)):
        lo = A_IN + g * B_BRANCH_IN
        cs[:, lo:lo + B_OUT] = HEAD_DIM ** -0.5
    return jnp.asarray(cs)


def kernel(x, rel_bias, attn_norm, mlp_norm, final_norm, w_in_ab, sinks, w_out_ab, w_down_c, q_norm_c,
           w_uq_c, kv_norm_c, w_ukv_c, w_o_c, w_mlp_up, w_mlp_down):
    bn, s, d = x.shape
    depth = attn_norm.shape[0]
    n = bn * s
    bias_all = _bias_table(rel_bias)
    colscale = _even_colscale()
    tables = _rope_tables(s)
    row = lambda v: v.reshape(1, -1)
    h = x.reshape(n, d)
    for layer in range(depth):
        gf = row(final_norm) if layer == depth - 1 else None
        mlp = (row(mlp_norm[layer]), w_mlp_up[layer].astype(BF16), w_mlp_down[layer].astype(BF16), gf)
        if layer % 2 == 0:
            e = layer // 2
            pa, pb0, pb1, pb2 = _even_proj(h, row(attn_norm[layer]), w_in_ab[e].astype(BF16), colscale,
                                           bn, s)
            oa = _band_a(pa.reshape(bn, s, A_IN), sinks[e], bias_all)
            ob0 = _band_b(pb0.reshape(bn, 1, s, B_BRANCH_IN), bias_all, 0)
            ob1 = _band_b(pb1, bias_all, 1)
            ob2 = _band_b(pb2, bias_all, 2)
            h = _post_even(h, oa.reshape(n, A_OUT), ob0.reshape(n, 2 * B_OUT), ob1, ob2, s,
                           w_out_ab[e].astype(BF16), *mlp)
        else:
            o = layer // 2
            wd_ext, wq, wk, wvt = _mla_weights(w_down_c[o], w_uq_c[o], w_ukv_c[o])
            q, k, vt = _mla_proj(h, row(attn_norm[layer]), wd_ext, row(q_norm_c[o]), wq,
                                 row(kv_norm_c[o]), wk, wvt, tables, s)
            att = _mla_attn(q.reshape(bn, s, C_PAD), k.reshape(bn, s, C_PAD), vt)
            h = _post_odd(h, att.reshape(n, C_HEADS * C_V), w_o_c[o].astype(BF16), *mlp)
    return h.reshape(bn, s, d)
```

```python
import functools
import math

import numpy as np
import jax
import jax.numpy as jnp
from jax import lax
from jax.experimental import pallas as pl
from jax.experimental.pallas import tpu as pltpu

F32 = jnp.float32
BF16 = jnp.bfloat16

D_MODEL = 1024
HEAD_DIM = 64
BLOCK = 128
EPS = 1e-6
NEG = -1e30
A_Q_HEADS = 8
A_KV_HEADS = 2
A_GROUP = A_Q_HEADS // A_KV_HEADS
A_WINDOW = 128
B_BRANCHES = ((128, 1), (512, 4), (2048, 16))
B_HEADS_PER_BRANCH = 4
NUM_BUCKETS = 32
MAX_DISTANCE = 2048
N_BIAS_HEADS = A_Q_HEADS + len(B_BRANCHES) * B_HEADS_PER_BRANCH
A_IN = (A_Q_HEADS + 2 * A_KV_HEADS) * HEAD_DIM
B_BRANCH_IN = 3 * B_HEADS_PER_BRANCH * HEAD_DIM
AB_IN = A_IN + len(B_BRANCHES) * B_BRANCH_IN
A_OUT = A_Q_HEADS * HEAD_DIM
B_OUT = B_HEADS_PER_BRANCH * HEAD_DIM
C_HEADS = 8
C_NOPE = 64
C_ROPE = 32
C_V = 64
C_Q_RANK = 384
C_KV_RANK = 256
ROPE_THETA = 10000.0
D_FF = 4 * D_MODEL

LANES = 128
BF16_ROWS = 16
C_PAD = C_HEADS * LANES
C_DOWN_EXT = C_Q_RANK + C_KV_RANK + LANES
ROPE_LO = C_NOPE
ROPE_HALF = C_ROPE // 2

VMEM_LIMIT = 52 * 1024 * 1024

TM_PROJ = 1024
PERM_GROUP = 256
TQ_BAND_A = 256
TQ_BAND = 512
TM_POST = 512
T_ATT = 512
TK_ATT = 256
assert T_ATT == 2 * TK_ATT

_NT = (((1,), (1,)), ((), ()))


def _const_spec(shape):
    nd = len(shape)
    return pl.BlockSpec(shape, lambda *_: (0,) * nd, pipeline_mode=pl.Buffered(1))


def _rms(x, g):
    ms = jnp.mean(x * x, axis=-1, keepdims=True)
    return x * lax.rsqrt(ms + EPS) * g


def _t5_bucket_np(n):
    max_exact = NUM_BUCKETS // 2
    nf = np.maximum(n, 1).astype(np.float32)
    large = max_exact + (np.log(nf / max_exact) / math.log(MAX_DISTANCE / max_exact)
                         * (NUM_BUCKETS - max_exact)).astype(np.int32)
    return np.where(n < max_exact, n, np.minimum(large, NUM_BUCKETS - 1))


def _band_bucket_index(dilation, max_dist):
    i = np.arange(BLOCK)[:, None]
    j = np.arange(2 * BLOCK)[None, :]
    dist = i + BLOCK - j
    inband = (dist >= 0) & (dist <= max_dist)
    return np.where(inband, _t5_bucket_np(np.maximum(dist, 0) * dilation), -1).astype(np.int32)


def _bias_kernel(tab_ref, idx_ref, o_ref):
    h = pl.program_id(0)
    idx = idx_ref[0]
    acc = jnp.full(idx.shape, NEG, F32)
    for b in range(NUM_BUCKETS):
        acc = jnp.where(idx == b, tab_ref[h * NUM_BUCKETS + b], acc)
    o_ref[0] = acc


def _bias_table(rel_bias):
    configs = [(1, A_WINDOW - 1)] + [(dil, window // dil) for window, dil in B_BRANCHES]
    idx = jnp.asarray(np.stack([_band_bucket_index(d, m) for d, m in configs]))
    tab = rel_bias.T.reshape(-1)

    def cfg(h):
        return jnp.where(h < A_Q_HEADS, 0, 1 + (h - A_Q_HEADS) // B_HEADS_PER_BRANCH)

    return pl.pallas_call(
        _bias_kernel,
        grid=(N_BIAS_HEADS,),
        in_specs=[pl.BlockSpec(memory_space=pltpu.SMEM),
                  pl.BlockSpec((1, BLOCK, 2 * BLOCK), lambda h: (cfg(h), 0, 0))],
        out_specs=pl.BlockSpec((1, BLOCK, 2 * BLOCK), lambda h: (h, 0, 0)),
        out_shape=jax.ShapeDtypeStruct((N_BIAS_HEADS, BLOCK, 2 * BLOCK), F32),
        name="bias_table",
    )(tab, idx)


def _deinterleave_matrix(dil):
    rho = np.arange(PERM_GROUP)
    per_res = PERM_GROUP // dil
    p = np.zeros((PERM_GROUP, PERM_GROUP), np.float32)
    p[rho, (rho % per_res) * dil + rho // per_res] = 1.0
    return jnp.asarray(p, BF16)


def _even_proj_kernel(x_ref, g_ref, w_ref, cs_ref, p1_ref, p2_ref, pa_ref, pb0_ref, pb1_ref, pb2_ref):
    xn = _rms(x_ref[...], g_ref[...]).astype(BF16)
    y = (jnp.dot(xn, w_ref[...], preferred_element_type=F32) * cs_ref[...]).astype(BF16)
    pa_ref[...] = y[:, :A_IN]
    pb0_ref[...] = y[:, A_IN:A_IN + B_BRANCH_IN]
    for g, (perm_ref, out_ref) in enumerate(((p1_ref, pb1_ref), (p2_ref, pb2_ref)), start=1):
        dil = B_BRANCHES[g][1]
        rows = PERM_GROUP // dil
        lo = A_IN + g * B_BRANCH_IN
        for grp in range(TM_PROJ // PERM_GROUP):
            yg = y[grp * PERM_GROUP:(grp + 1) * PERM_GROUP, lo:lo + B_BRANCH_IN]
            z = jnp.dot(perm_ref[...], yg, preferred_element_type=F32).astype(BF16)
            for r in range(dil):
                out_ref[0, r, grp * rows:(grp + 1) * rows, :] = z[r * rows:(r + 1) * rows]


def _even_proj(h, g, w, colscale, bn, seq):
    n, d = h.shape
    per_seq = seq // TM_PROJ
    tile = lambda w_: pl.BlockSpec((TM_PROJ, w_), lambda i: (i, 0))
    dils = [dil for _, dil in B_BRANCHES[1:]]
    dil_spec = lambda dil: pl.BlockSpec((1, dil, TM_PROJ // dil, B_BRANCH_IN),
                                        lambda i: (i // per_seq, 0, i % per_seq, 0))
    perms = [_deinterleave_matrix(dil) for dil in dils]
    return pl.pallas_call(
        _even_proj_kernel,
        grid=(n // TM_PROJ,),
        in_specs=[tile(d), _const_spec((1, d)), _const_spec(w.shape), _const_spec(colscale.shape)]
        + [_const_spec(p.shape) for p in perms],
        out_specs=[tile(A_IN), tile(B_BRANCH_IN)] + [dil_spec(dil) for dil in dils],
        out_shape=[jax.ShapeDtypeStruct((n, A_IN), BF16), jax.ShapeDtypeStruct((n, B_BRANCH_IN), BF16)]
        + [jax.ShapeDtypeStruct((bn, dil, seq // dil, B_BRANCH_IN), BF16) for dil in dils],
        compiler_params=pltpu.CompilerParams(dimension_semantics=("arbitrary",),
                                             vmem_limit_bytes=VMEM_LIMIT),
        name="even_proj",
    )(h, g, w, colscale, *perms)


def _band_head(q, k, v, bias, first_mask, sink):
    s = lax.dot_general(q, k, _NT, preferred_element_type=F32) + bias
    if first_mask is not None:
        s = jnp.where(first_mask, NEG, s)
    m = jnp.max(s, axis=-1, keepdims=True)
    if sink is not None:
        m = jnp.maximum(m, sink)
    p = jnp.exp(s - m)
    l = jnp.sum(p, axis=-1, keepdims=True)
    if sink is not None:
        l = l + jnp.exp(sink - m)
    o = jnp.dot(p.astype(BF16), v, preferred_element_type=F32)
    return o / l, m, l


def _first_block_mask(is_first_tile):
    col = lax.broadcasted_iota(jnp.int32, (BLOCK, 2 * BLOCK), 1)
    return jnp.logical_and(is_first_tile, col < BLOCK)


def _with_prev(cur, prev, lo, width):
    return jnp.concatenate([prev[:, lo:lo + width], cur[:, lo:lo + width]], axis=0)


def _band_a_kernel(sink_ref, cur_ref, prev_ref, bias_ref, o_ref):
    is_first = pl.program_id(1) == 0
    cur, prev = cur_ref[0], prev_ref[0]
    kd = A_KV_HEADS * HEAD_DIM
    k_all = _with_prev(cur, prev, A_OUT, kd)
    v_all = _with_prev(cur, prev, A_OUT + kd, kd)
    for j in range(TQ_BAND_A // BLOCK):
        rows = slice(j * BLOCK, (j + 1) * BLOCK)
        krows = slice(j * BLOCK, (j + 2) * BLOCK)
        mask = _first_block_mask(is_first) if j == 0 else None
        outs = []
        for h in range(A_Q_HEADS):
            kv = slice((h // A_GROUP) * HEAD_DIM, (h // A_GROUP + 1) * HEAD_DIM)
            o, _, _ = _band_head(cur[rows, h * HEAD_DIM:(h + 1) * HEAD_DIM], k_all[krows, kv],
                                 v_all[krows, kv], bias_ref[h], mask, sink_ref[h])
            outs.append(o)
        o_ref[0, rows, :] = jnp.concatenate(outs, axis=-1).astype(o_ref.dtype)


def _band_a(pa, sinks, bias_all):
    bn, s, _ = pa.shape
    per_tile = TQ_BAND_A // BLOCK
    return pl.pallas_call(
        _band_a_kernel,
        grid=(bn, s // TQ_BAND_A),
        in_specs=[pl.BlockSpec(memory_space=pltpu.SMEM),
                  pl.BlockSpec((1, TQ_BAND_A, A_IN), lambda b, n: (b, n, 0)),
                  pl.BlockSpec((1, BLOCK, A_IN), lambda b, n: (b, jnp.maximum(n * per_tile - 1, 0), 0)),
                  pl.BlockSpec((A_Q_HEADS, BLOCK, 2 * BLOCK), lambda b, n: (0, 0, 0))],
        out_specs=pl.BlockSpec((1, TQ_BAND_A, A_OUT), lambda b, n: (b, n, 0)),
        out_shape=jax.ShapeDtypeStruct((bn, s, A_OUT), BF16),
        compiler_params=pltpu.CompilerParams(dimension_semantics=("arbitrary", "arbitrary")),
        name="band_a",
    )(sinks, pa, pa, bias_all)


def _band_b_kernel(cur_ref, prev_ref, bias_ref, o_ref):
    is_first = pl.program_id(2) == 0
    cur, prev = cur_ref[0, 0], prev_ref[0, 0]
    k_all = _with_prev(cur, prev, B_OUT, B_OUT)
    v_all = _with_prev(cur, prev, 2 * B_OUT, B_OUT)
    for j in range(TQ_BAND // BLOCK):
        rows = slice(j * BLOCK, (j + 1) * BLOCK)
        krows = slice(j * BLOCK, (j + 2) * BLOCK)
        mask = _first_block_mask(is_first) if j == 0 else None
        outs, lses = [], []
        for h in range(B_HEADS_PER_BRANCH):
            sl = slice(h * HEAD_DIM, (h + 1) * HEAD_DIM)
            o, m, l = _band_head(cur[rows, sl], k_all[krows, sl], v_all[krows, sl], bias_ref[h], mask,
                                 None)
            outs.append(o)
            lses.append(jnp.broadcast_to(m + jnp.log(l), (BLOCK, HEAD_DIM)))
        o_ref[0, 0, rows, :] = jnp.concatenate(outs + lses, axis=-1)


def _band_b(pb, bias_all, g):
    bn, dil, l, _ = pb.shape
    per_tile = TQ_BAND // BLOCK
    bias_blk = (A_Q_HEADS + g * B_HEADS_PER_BRANCH) // B_HEADS_PER_BRANCH
    return pl.pallas_call(
        _band_b_kernel,
        grid=(bn, dil, l // TQ_BAND),
        in_specs=[pl.BlockSpec((1, 1, TQ_BAND, B_BRANCH_IN), lambda b, r, n: (b, r, n, 0)),
                  pl.BlockSpec((1, 1, BLOCK, B_BRANCH_IN),
                               lambda b, r, n: (b, r, jnp.maximum(n * per_tile - 1, 0), 0)),
                  pl.BlockSpec((B_HEADS_PER_BRANCH, BLOCK, 2 * BLOCK), lambda b, r, n: (bias_blk, 0, 0))],
        out_specs=pl.BlockSpec((1, 1, TQ_BAND, 2 * B_OUT), lambda b, r, n: (b, r, n, 0)),
        out_shape=jax.ShapeDtypeStruct((bn, dil, l, 2 * B_OUT), F32),
        compiler_params=pltpu.CompilerParams(
            dimension_semantics=("arbitrary", "arbitrary", "arbitrary")),
        name=f"band_b{g}",
    )(pb, pb, bias_all)


def _mlp_tail(h2, gm_ref, wu_ref, wd_ref, gf_ref, o_ref):
    xn = _rms(h2, gm_ref[...]).astype(BF16)
    u = jnp.maximum(jnp.dot(xn, wu_ref[...], preferred_element_type=F32), 0.0)
    u = (u * u).astype(BF16)
    h3 = h2 + jnp.dot(u, wd_ref[...], preferred_element_type=F32)
    if gf_ref is not None:
        h3 = _rms(h3, gf_ref[...])
    o_ref[...] = h3


def _token_major_slabs(ob_ref, scr_ref):
    dil = ob_ref.shape[1]
    rows = TM_POST // dil
    n_slabs = ob_ref.shape[3] // LANES
    for r in range(dil):
        blk = ob_ref[0, r]
        for c in range(n_slabs):
            scr_ref[c, pl.ds(r, rows, stride=dil), :] = blk[:, c * LANES:(c + 1) * LANES]
    return [scr_ref[c] for c in range(n_slabs)]


def _post_even_kernel(final, h_ref, oa_ref, b0_ref, b1_ref, b2_ref, wo_ref, gm_ref, wu_ref, wd_ref,
                      *rest):
    gf_ref = rest[0] if final else None
    o_ref, scr1_ref, scr2_ref = rest[-3:]
    b0 = b0_ref[...]
    slabs = [[b0[:, c * LANES:(c + 1) * LANES] for c in range(2 * B_OUT // LANES)],
             _token_major_slabs(b1_ref, scr1_ref), _token_major_slabs(b2_ref, scr2_ref)]
    n_o = B_OUT // LANES
    merged = []
    for c in range(n_o):
        lses = [sl[n_o + c] for sl in slabs]
        mx = jnp.maximum(jnp.maximum(lses[0], lses[1]), lses[2])
        es = [jnp.exp(ls - mx) for ls in lses]
        num = es[0] * slabs[0][c] + es[1] * slabs[1][c] + es[2] * slabs[2][c]
        merged.append(num / (es[0] + es[1] + es[2]))
    out_b = jnp.concatenate(merged, axis=-1).astype(BF16)
    mix = jnp.dot(oa_ref[...], wo_ref[:A_OUT, :], preferred_element_type=F32)
    mix = mix + jnp.dot(out_b, wo_ref[A_OUT:, :], preferred_element_type=F32)
    _mlp_tail(h_ref[...] + mix, gm_ref, wu_ref, wd_ref, gf_ref, o_ref)


def _post_odd_kernel(final, h_ref, a_ref, wo_ref, gm_ref, wu_ref, wd_ref, *rest):
    gf_ref, o_ref = (rest[0], rest[1]) if final else (None, rest[0])
    mix = jnp.dot(a_ref[...], wo_ref[...], preferred_element_type=F32)
    _mlp_tail(h_ref[...] + mix, gm_ref, wu_ref, wd_ref, gf_ref, o_ref)


def _post(kernel_fn, name, h, acts, act_specs, scratch, wo, gm, wu, wd, gf):
    n, d = h.shape
    final = gf is not None
    params = [wo, gm, wu, wd] + ([gf] if final else [])
    return pl.pallas_call(
        functools.partial(kernel_fn, final),
        grid=(n // TM_POST,),
        in_specs=[_row_tile(d)] + act_specs + [_const_spec(p.shape) for p in params],
        out_specs=_row_tile(d),
        out_shape=jax.ShapeDtypeStruct((n, d), F32),
        scratch_shapes=scratch,
        compiler_params=pltpu.CompilerParams(dimension_semantics=("arbitrary",),
                                             vmem_limit_bytes=VMEM_LIMIT),
        name=name,
    )(h, *acts, *params)


def _row_tile(width):
    return pl.BlockSpec((TM_POST, width), lambda i: (i, 0))


def _post_even(h, oa, ob0, ob1, ob2, seq, *params):
    per_seq = seq // TM_POST
    res_spec = lambda ob: pl.BlockSpec((1, ob.shape[1], TM_POST // ob.shape[1], ob.shape[3]),
                                       lambda i: (i // per_seq, 0, i % per_seq, 0))
    specs = [_row_tile(oa.shape[1]), _row_tile(ob0.shape[1]), res_spec(ob1), res_spec(ob2)]
    scratch = [pltpu.VMEM((2 * B_OUT // LANES, TM_POST, LANES), F32)] * 2
    return _post(_post_even_kernel, "post_even", h, [oa, ob0, ob1, ob2], specs, scratch, *params)


def _post_odd(h, att, *params):
    return _post(_post_odd_kernel, "post_odd", h, [att], [_row_tile(att.shape[1])], [], *params)


def _rope_lanes(x, cos, s1, s2):
    return x * cos + pltpu.roll(x, LANES - ROPE_HALF, 1) * s1 + pltpu.roll(x, ROPE_HALF, 1) * s2


def _mla_proj_kernel(scale, h_ref, g_ref, wd_ref, qn_ref, wq_ref, kn_ref, wk_ref, wv_ref,
                     cos_ref, s1_ref, s2_ref, q_ref, k_ref, vt_ref):
    xn = _rms(h_ref[...], g_ref[...]).astype(BF16)
    down = jnp.dot(xn, wd_ref[...], preferred_element_type=F32)
    cq = _rms(down[:, :C_Q_RANK], qn_ref[...]).astype(BF16)
    ckv = _rms(down[:, C_Q_RANK:C_Q_RANK + C_KV_RANK], kn_ref[...]).astype(BF16)
    cos, s1, s2 = cos_ref[...], s1_ref[...], s2_ref[...]
    kr = _rope_lanes(down[:, C_Q_RANK + C_KV_RANK:], cos, s1, s2)
    q = jnp.dot(cq, wq_ref[...], preferred_element_type=F32)
    k = jnp.dot(ckv, wk_ref[...], preferred_element_type=F32)
    lane = lax.broadcasted_iota(jnp.int32, (1, LANES), 1)
    cos_q = jnp.where(lane < C_NOPE, 1.0, cos)
    for hh in range(C_HEADS):
        sl = slice(hh * LANES, (hh + 1) * LANES)
        q_ref[:, sl] = (_rope_lanes(q[:, sl], cos_q, s1, s2) * scale).astype(q_ref.dtype)
        k_ref[:, sl] = (k[:, sl] + kr).astype(k_ref.dtype)
    vt_ref[0] = lax.dot_general(wv_ref[...], ckv, _NT, preferred_element_type=F32).astype(vt_ref.dtype)


def _rope_tables(s):
    inv = ROPE_THETA ** (-jnp.arange(0, C_ROPE, 2, dtype=F32) / C_ROPE)
    ang = jnp.arange(s, dtype=F32)[:, None] * inv[None, :]
    cos, sin = jnp.cos(ang), jnp.sin(ang)
    z = lambda w: jnp.zeros((s, w), F32)
    tail = LANES - ROPE_LO - C_ROPE
    cos_t = jnp.concatenate([z(ROPE_LO), cos, cos, z(tail)], axis=1)
    s1_t = jnp.concatenate([z(ROPE_LO), -sin, z(ROPE_HALF), z(tail)], axis=1)
    s2_t = jnp.concatenate([z(ROPE_LO), z(ROPE_HALF), sin, z(tail)], axis=1)
    return cos_t, s1_t, s2_t


def _pad_heads(w, width):
    r = w.shape[0]
    w3 = jnp.pad(w.reshape(r, C_HEADS, width), ((0, 0), (0, 0), (0, LANES - width)))
    return w3.reshape(r, C_PAD)


def _mla_weights(w_down, w_uq, w_ukv):
    kr = jnp.zeros((D_MODEL, LANES), w_down.dtype).at[:, ROPE_LO:ROPE_LO + C_ROPE].set(
        w_down[:, C_Q_RANK + C_KV_RANK:])
    wd_ext = jnp.concatenate([w_down[:, :C_Q_RANK + C_KV_RANK], kr], axis=1)
    wq = _pad_heads(w_uq, C_NOPE + C_ROPE)
    ukv = w_ukv.reshape(C_KV_RANK, C_HEADS, C_NOPE + C_V)
    wk = _pad_heads(ukv[:, :, :C_NOPE].reshape(C_KV_RANK, -1), C_NOPE)
    wvt = ukv[:, :, C_NOPE:].reshape(C_KV_RANK, -1).T
    return wd_ext.astype(BF16), wq.astype(BF16), wk.astype(BF16), wvt.astype(BF16)


def _mla_proj(h, g, wd_ext, qn, wq, kn, wk, wvt, tables, seq):
    n, d = h.shape
    scale = (C_NOPE + C_ROPE) ** -0.5 * math.log2(math.e)
    per_seq = seq // TM_PROJ
    tab_spec = pl.BlockSpec((TM_PROJ, LANES), lambda i: (i % per_seq, 0))
    tile = lambda w: pl.BlockSpec((TM_PROJ, w), lambda i: (i, 0))
    vt_rows = C_HEADS * C_V
    vt_spec = pl.BlockSpec((1, vt_rows, TM_PROJ), lambda i: (i // per_seq, 0, i % per_seq))
    params = [g, wd_ext, qn, wq, kn, wk, wvt]
    return pl.pallas_call(
        functools.partial(_mla_proj_kernel, scale),
        grid=(n // TM_PROJ,),
        in_specs=[tile(d)] + [_const_spec(p.shape) for p in params] + [tab_spec] * 3,
        out_specs=[tile(C_PAD), tile(C_PAD), vt_spec],
        out_shape=[jax.ShapeDtypeStruct((n, C_PAD), BF16)] * 2
        + [jax.ShapeDtypeStruct((n // seq, vt_rows, seq), BF16)],
        compiler_params=pltpu.CompilerParams(dimension_semantics=("arbitrary",),
                                             vmem_limit_bytes=VMEM_LIMIT),
        name="mla_proj",
    )(h, *params, *tables)


def _mla_attn_kernel(q_ref, k_ref, vt_ref, o_ref, sa_ref, sb_ref):
    i = pl.program_id(2)
    tq, tk = T_ATT, TK_ATT
    per_q = tq // tk
    qt = q_ref[0].T
    qs = (qt[:LANES], qt[LANES:])

    def produce(chunk, buf, mask):
        off = pl.multiple_of(chunk * tk, tk)
        mx = []
        for hh in range(2):
            k = k_ref[0, pl.ds(off, tk), hh * LANES:(hh + 1) * LANES]
            s = jnp.dot(k, qs[hh], preferred_element_type=F32)
            if mask is not None:
                s = jnp.where(mask, s, NEG)
            buf[hh] = s
            mx.append(jnp.max(s, axis=0, keepdims=True))
        return tuple(mx)

    def consume(chunk, buf, mx, state):
        off = pl.multiple_of(chunk * tk, tk)
        new = []
        for hh in range(2):
            m, acc = state[hh]
            vt = vt_ref[0, hh * C_V:(hh + 1) * C_V, pl.ds(off, tk)]
            m_new = jnp.maximum(m, mx[hh])
            p = jnp.exp2(buf[hh] - m_new).astype(BF16)
            pv = jnp.dot(jnp.concatenate([vt, ones], axis=0), p, preferred_element_type=F32)
            new.append((m_new, jnp.exp2(m - m_new) * acc + pv))
        return tuple(new)

    key = lax.broadcasted_iota(jnp.int32, (tk, tq), 0)
    qry = lax.broadcasted_iota(jnp.int32, (tk, tq), 1)
    d0 = i * per_q
    ones = jnp.ones((BF16_ROWS, tk), BF16)
    head0 = (jnp.full((1, tq), NEG, F32), jnp.zeros((C_V + BF16_ROWS, tq), F32))
    mx_a = produce(d0, sa_ref, key <= qry)
    mx_b = produce(d0 + 1, sb_ref, key + tk <= qry)
    state = consume(d0, sa_ref, mx_a, (head0, head0))

    def body(jj, carry):
        mx_b, state = carry
        pend = jnp.where(jj == 0, d0 + 1, 2 * jj - 1)
        mx_a = produce(2 * jj, sa_ref, None)
        state = consume(pend, sb_ref, mx_b, state)
        mx_b = produce(2 * jj + 1, sb_ref, None)
        state = consume(2 * jj, sa_ref, mx_a, state)
        return mx_b, state

    mx_b, state = lax.fori_loop(0, i, body, (mx_b, state))
    pend = jnp.where(i == 0, d0 + 1, 2 * i - 1)
    (_, a0), (_, a1) = consume(pend, sb_ref, mx_b, state)
    o = jnp.concatenate([a[:C_V] / a[C_V:C_V + 1] for a in (a0, a1)], axis=0)
    o_ref[0] = o.T.astype(o_ref.dtype)


def _mla_attn(q, k, vt):
    bn, s, _ = q.shape
    pair = 2 * LANES
    return pl.pallas_call(
        _mla_attn_kernel,
        grid=(bn, C_HEADS // 2, s // T_ATT),
        in_specs=[pl.BlockSpec((1, T_ATT, pair), lambda b, p, i: (b, i, p)),
                  pl.BlockSpec((1, s, pair), lambda b, p, i: (b, 0, p)),
                  pl.BlockSpec((1, 2 * C_V, s), lambda b, p, i: (b, p, 0))],
        out_specs=pl.BlockSpec((1, T_ATT, 2 * C_V), lambda b, p, i: (b, i, p)),
        out_shape=jax.ShapeDtypeStruct((bn, s, C_HEADS * C_V), BF16),
        scratch_shapes=[pltpu.VMEM((2, TK_ATT, T_ATT), F32)] * 2,
        compiler_params=pltpu.CompilerParams(
            dimension_semantics=("arbitrary", "arbitrary", "arbitrary"),
            vmem_limit_bytes=VMEM_LIMIT),
        name="mla_attn",
    )(q, k, vt)


def _even_colscale():
    cs = np.ones((1, AB_IN), np.float32)
    cs[:, :A_OUT] = HEAD_DIM ** -0.5
    for g in range(len(B_BRANCHES)):
        lo = A_IN + g * B_BRANCH_IN
        cs[:, lo:lo + B_OUT] = HEAD_DIM ** -0.5
    return jnp.asarray(cs)


def kernel(x, rel_bias, attn_norm, mlp_norm, final_norm, w_in_ab, sinks, w_out_ab, w_down_c, q_norm_c,
           w_uq_c, kv_norm_c, w_ukv_c, w_o_c, w_mlp_up, w_mlp_down):
    bn, s, d = x.shape
    depth = attn_norm.shape[0]
    n = bn * s
    bias_all = _bias_table(rel_bias)
    colscale = _even_colscale()
    tables = _rope_tables(s)
    row = lambda v: v.reshape(1, -1)
    h = x.reshape(n, d)
    for layer in range(depth):
        gf = row(final_norm) if layer == depth - 1 else None
        mlp = (row(mlp_norm[layer]), w_mlp_up[layer].astype(BF16), w_mlp_down[layer].astype(BF16), gf)
        if layer % 2 == 0:
            e = layer // 2
            pa, pb0, pb1, pb2 = _even_proj(h, row(attn_norm[layer]), w_in_ab[e].astype(BF16), colscale,
                                           bn, s)
            oa = _band_a(pa.reshape(bn, s, A_IN), sinks[e], bias_all)
            ob0 = _band_b(pb0.reshape(bn, 1, s, B_BRANCH_IN), bias_all, 0)
            ob1 = _band_b(pb1, bias_all, 1)
            ob2 = _band_b(pb2, bias_all, 2)
            h = _post_even(h, oa.reshape(n, A_OUT), ob0.reshape(n, 2 * B_OUT), ob1, ob2, s,
                           w_out_ab[e].astype(BF16), *mlp)
        else:
            o = layer // 2
            wd_ext, wq, wk, wvt = _mla_weights(w_down_c[o], w_uq_c[o], w_ukv_c[o])
            q, k, vt = _mla_proj(h, row(attn_norm[layer]), wd_ext, row(q_norm_c[o]), wq,
                                 row(kv_norm_c[o]), wk, wvt, tables, s)
            att = _mla_attn(q.reshape(bn, s, C_PAD), k.reshape(bn, s, C_PAD), vt)
            h = _post_odd(h, att.reshape(n, C_HEADS * C_V), w_o_c[o].astype(BF16), *mlp)
    return h.reshape(bn, s, d)
```

```python
import functools
import math

import numpy as np
import jax
import jax.numpy as jnp
from jax import lax
from jax.experimental import pallas as pl
from jax.experimental.pallas import tpu as pltpu

F32 = jnp.float32
BF16 = jnp.bfloat16

D_MODEL = 1024
HEAD_DIM = 64
BLOCK = 128
EPS = 1e-6
NEG = -1e30
A_Q_HEADS = 8
A_KV_HEADS = 2
A_GROUP = A_Q_HEADS // A_KV_HEADS
A_WINDOW = 128
B_BRANCHES = ((128, 1), (512, 4), (2048, 16))
B_HEADS_PER_BRANCH = 4
NUM_BUCKETS = 32
MAX_DISTANCE = 2048
N_BIAS_HEADS = A_Q_HEADS + len(B_BRANCHES) * B_HEADS_PER_BRANCH
A_IN = (A_Q_HEADS + 2 * A_KV_HEADS) * HEAD_DIM
B_BRANCH_IN = 3 * B_HEADS_PER_BRANCH * HEAD_DIM
AB_IN = A_IN + len(B_BRANCHES) * B_BRANCH_IN
A_OUT = A_Q_HEADS * HEAD_DIM
B_OUT = B_HEADS_PER_BRANCH * HEAD_DIM
C_HEADS = 8
C_NOPE = 64
C_ROPE = 32
C_V = 64
C_Q_RANK = 384
C_KV_RANK = 256
ROPE_THETA = 10000.0
D_FF = 4 * D_MODEL

LANES = 128
BF16_ROWS = 16
C_PAD = C_HEADS * LANES
C_DOWN_EXT = C_Q_RANK + C_KV_RANK + LANES
ROPE_LO = C_NOPE
ROPE_HALF = C_ROPE // 2

VMEM_LIMIT = 52 * 1024 * 1024

TM_PROJ = 1024
PERM_GROUP = 256
TQ_BAND_A = 256
TQ_BAND = 512
TM_POST = 512
T_ATT = 512
TK_ATT = 256
assert T_ATT == 2 * TK_ATT

_NT = (((1,), (1,)), ((), ()))


def _const_spec(shape):
    nd = len(shape)
    return pl.BlockSpec(shape, lambda *_: (0,) * nd, pipeline_mode=pl.Buffered(1))


def _rms(x, g):
    ms = jnp.mean(x * x, axis=-1, keepdims=True)
    return x * lax.rsqrt(ms + EPS) * g


def _t5_bucket_np(n):
    max_exact = NUM_BUCKETS // 2
    nf = np.maximum(n, 1).astype(np.float32)
    large = max_exact + (np.log(nf / max_exact) / math.log(MAX_DISTANCE / max_exact)
                         * (NUM_BUCKETS - max_exact)).astype(np.int32)
    return np.where(n < max_exact, n, np.minimum(large, NUM_BUCKETS - 1))


def _band_bucket_index(dilation, max_dist):
    i = np.arange(BLOCK)[:, None]
    j = np.arange(2 * BLOCK)[None, :]
    dist = i + BLOCK - j
    inband = (dist >= 0) & (dist <= max_dist)
    return np.where(inband, _t5_bucket_np(np.maximum(dist, 0) * dilation), -1).astype(np.int32)


def _bias_kernel(tab_ref, idx_ref, o_ref):
    h = pl.program_id(0)
    idx = idx_ref[0]
    acc = jnp.full(idx.shape, NEG, F32)
    for b in range(NUM_BUCKETS):
        acc = jnp.where(idx == b, tab_ref[h * NUM_BUCKETS + b], acc)
    o_ref[0] = acc


def _bias_table(rel_bias):
    configs = [(1, A_WINDOW - 1)] + [(dil, window // dil) for window, dil in B_BRANCHES]
    idx = jnp.asarray(np.stack([_band_bucket_index(d, m) for d, m in configs]))
    tab = rel_bias.T.reshape(-1)

    def cfg(h):
        return jnp.where(h < A_Q_HEADS, 0, 1 + (h - A_Q_HEADS) // B_HEADS_PER_BRANCH)

    return pl.pallas_call(
        _bias_kernel,
        grid=(N_BIAS_HEADS,),
        in_specs=[pl.BlockSpec(memory_space=pltpu.SMEM),
                  pl.BlockSpec((1, BLOCK, 2 * BLOCK), lambda h: (cfg(h), 0, 0))],
        out_specs=pl.BlockSpec((1, BLOCK, 2 * BLOCK), lambda h: (h, 0, 0)),
        out_shape=jax.ShapeDtypeStruct((N_BIAS_HEADS, BLOCK, 2 * BLOCK), F32),
        name="bias_table",
    )(tab, idx)


def _deinterleave_matrix(dil):
    rho = np.arange(PERM_GROUP)
    per_res = PERM_GROUP // dil
    p = np.zeros((PERM_GROUP, PERM_GROUP), np.float32)
    p[rho, (rho % per_res) * dil + rho // per_res] = 1.0
    return jnp.asarray(p, BF16)


def _even_proj_kernel(x_ref, g_ref, w_ref, cs_ref, p1_ref, p2_ref, pa_ref, pb0_ref, pb1_ref, pb2_ref):
    xn = _rms(x_ref[...], g_ref[...]).astype(BF16)
    y = (jnp.dot(xn, w_ref[...], preferred_element_type=F32) * cs_ref[...]).astype(BF16)
    pa_ref[...] = y[:, :A_IN]
    pb0_ref[...] = y[:, A_IN:A_IN + B_BRANCH_IN]
    for g, (perm_ref, out_ref) in enumerate(((p1_ref, pb1_ref), (p2_ref, pb2_ref)), start=1):
        dil = B_BRANCHES[g][1]
        rows = PERM_GROUP // dil
        lo = A_IN + g * B_BRANCH_IN
        for grp in range(TM_PROJ // PERM_GROUP):
            yg = y[grp * PERM_GROUP:(grp + 1) * PERM_GROUP, lo:lo + B_BRANCH_IN]
            z = jnp.dot(perm_ref[...], yg, preferred_element_type=F32).astype(BF16)
            for r in range(dil):
                out_ref[0, r, grp * rows:(grp + 1) * rows, :] = z[r * rows:(r + 1) * rows]


def _even_proj(h, g, w, colscale, bn, seq):
    n, d = h.shape
    per_seq = seq // TM_PROJ
    tile = lambda w_: pl.BlockSpec((TM_PROJ, w_), lambda i: (i, 0))
    dils = [dil for _, dil in B_BRANCHES[1:]]
    dil_spec = lambda dil: pl.BlockSpec((1, dil, TM_PROJ // dil, B_BRANCH_IN),
                                        lambda i: (i // per_seq, 0, i % per_seq, 0))
    perms = [_deinterleave_matrix(dil) for dil in dils]
    return pl.pallas_call(
        _even_proj_kernel,
        grid=(n // TM_PROJ,),
        in_specs=[tile(d), _const_spec((1, d)), _const_spec(w.shape), _const_spec(colscale.shape)]
        + [_const_spec(p.shape) for p in perms],
        out_specs=[tile(A_IN), tile(B_BRANCH_IN)] + [dil_spec(dil) for dil in dils],
        out_shape=[jax.ShapeDtypeStruct((n, A_IN), BF16), jax.ShapeDtypeStruct((n, B_BRANCH_IN), BF16)]
        + [jax.ShapeDtypeStruct((bn, dil, seq // dil, B_BRANCH_IN), BF16) for dil in dils],
        compiler_params=pltpu.CompilerParams(dimension_semantics=("arbitrary",),
                                             vmem_limit_bytes=VMEM_LIMIT),
        name="even_proj",
    )(h, g, w, colscale, *perms)


def _band_head(q, k, v, bias, first_mask, sink):
    s = lax.dot_general(q, k, _NT, preferred_element_type=F32) + bias
    if first_mask is not None:
        s = jnp.where(first_mask, NEG, s)
    m = jnp.max(s, axis=-1, keepdims=True)
    if sink is not None:
        m = jnp.maximum(m, sink)
    p = jnp.exp(s - m)
    l = jnp.sum(p, axis=-1, keepdims=True)
    if sink is not None:
        l = l + jnp.exp(sink - m)
    o = jnp.dot(p.astype(BF16), v, preferred_element_type=F32)
    return o / l, m, l


def _first_block_mask(is_first_tile):
    col = lax.broadcasted_iota(jnp.int32, (BLOCK, 2 * BLOCK), 1)
    return jnp.logical_and(is_first_tile, col < BLOCK)


def _with_prev(cur, prev, lo, width):
    return jnp.concatenate([prev[:, lo:lo + width], cur[:, lo:lo + width]], axis=0)


def _band_a_kernel(sink_ref, cur_ref, prev_ref, bias_ref, o_ref):
    is_first = pl.program_id(1) == 0
    cur, prev = cur_ref[0], prev_ref[0]
    kd = A_KV_HEADS * HEAD_DIM
    q_heads = [cur[:, h * HEAD_DIM:(h + 1) * HEAD_DIM] for h in range(A_Q_HEADS)]
    kv_heads = [(_with_prev(cur, prev, A_OUT + g * HEAD_DIM, HEAD_DIM),
                 _with_prev(cur, prev, A_OUT + kd + g * HEAD_DIM, HEAD_DIM)) for g in range(A_KV_HEADS)]
    for j in range(TQ_BAND_A // BLOCK):
        rows = slice(j * BLOCK, (j + 1) * BLOCK)
        krows = slice(j * BLOCK, (j + 2) * BLOCK)
        mask = _first_block_mask(is_first) if j == 0 else None
        outs = []
        for h in range(A_Q_HEADS):
            k_g, v_g = kv_heads[h // A_GROUP]
            o, _, _ = _band_head(q_heads[h][rows], k_g[krows], v_g[krows], bias_ref[h], mask, sink_ref[h])
            outs.append(o)
        o_ref[0, rows, :] = jnp.concatenate(outs, axis=-1).astype(o_ref.dtype)


def _band_a(pa, sinks, bias_all):
    bn, s, _ = pa.shape
    per_tile = TQ_BAND_A // BLOCK
    return pl.pallas_call(
        _band_a_kernel,
        grid=(bn, s // TQ_BAND_A),
        in_specs=[pl.BlockSpec(memory_space=pltpu.SMEM),
                  pl.BlockSpec((1, TQ_BAND_A, A_IN), lambda b, n: (b, n, 0)),
                  pl.BlockSpec((1, BLOCK, A_IN), lambda b, n: (b, jnp.maximum(n * per_tile - 1, 0), 0)),
                  pl.BlockSpec((A_Q_HEADS, BLOCK, 2 * BLOCK), lambda b, n: (0, 0, 0))],
        out_specs=pl.BlockSpec((1, TQ_BAND_A, A_OUT), lambda b, n: (b, n, 0)),
        out_shape=jax.ShapeDtypeStruct((bn, s, A_OUT), BF16),
        compiler_params=pltpu.CompilerParams(dimension_semantics=("arbitrary", "arbitrary")),
        name="band_a",
    )(sinks, pa, pa, bias_all)


def _band_b_kernel(cur_ref, prev_ref, bias_ref, o_ref):
    is_first = pl.program_id(2) == 0
    cur, prev = cur_ref[0, 0], prev_ref[0, 0]
    heads = []
    for h in range(B_HEADS_PER_BRANCH):
        lo = h * HEAD_DIM
        heads.append((cur[:, lo:lo + HEAD_DIM], _with_prev(cur, prev, B_OUT + lo, HEAD_DIM),
                      _with_prev(cur, prev, 2 * B_OUT + lo, HEAD_DIM)))
    for j in range(TQ_BAND // BLOCK):
        rows = slice(j * BLOCK, (j + 1) * BLOCK)
        krows = slice(j * BLOCK, (j + 2) * BLOCK)
        mask = _first_block_mask(is_first) if j == 0 else None
        outs, lses = [], []
        for h, (q_h, k_h, v_h) in enumerate(heads):
            o, m, l = _band_head(q_h[rows], k_h[krows], v_h[krows], bias_ref[h], mask, None)
            outs.append(o)
            lses.append(jnp.broadcast_to(m + jnp.log(l), (BLOCK, HEAD_DIM)))
        o_ref[0, 0, rows, :] = jnp.concatenate(outs + lses, axis=-1)


def _band_b(pb, bias_all, g):
    bn, dil, l, _ = pb.shape
    per_tile = TQ_BAND // BLOCK
    bias_blk = (A_Q_HEADS + g * B_HEADS_PER_BRANCH) // B_HEADS_PER_BRANCH
    return pl.pallas_call(
        _band_b_kernel,
        grid=(bn, dil, l // TQ_BAND),
        in_specs=[pl.BlockSpec((1, 1, TQ_BAND, B_BRANCH_IN), lambda b, r, n: (b, r, n, 0)),
                  pl.BlockSpec((1, 1, BLOCK, B_BRANCH_IN),
                               lambda b, r, n: (b, r, jnp.maximum(n * per_tile - 1, 0), 0)),
                  pl.BlockSpec((B_HEADS_PER_BRANCH, BLOCK, 2 * BLOCK), lambda b, r, n: (bias_blk, 0, 0))],
        out_specs=pl.BlockSpec((1, 1, TQ_BAND, 2 * B_OUT), lambda b, r, n: (b, r, n, 0)),
        out_shape=jax.ShapeDtypeStruct((bn, dil, l, 2 * B_OUT), F32),
        compiler_params=pltpu.CompilerParams(
            dimension_semantics=("arbitrary", "arbitrary", "arbitrary")),
        name=f"band_b{g}",
    )(pb, pb, bias_all)


def _mlp_tail(h2, gm_ref, wu_ref, wd_ref, gf_ref, o_ref):
    xn = _rms(h2, gm_ref[...]).astype(BF16)
    u = jnp.maximum(jnp.dot(xn, wu_ref[...], preferred_element_type=F32), 0.0)
    u = (u * u).astype(BF16)
    h3 = h2 + jnp.dot(u, wd_ref[...], preferred_element_type=F32)
    if gf_ref is not None:
        h3 = _rms(h3, gf_ref[...])
    o_ref[...] = h3


def _token_major_slabs(ob_ref, scr_ref):
    dil = ob_ref.shape[1]
    rows = TM_POST // dil
    n_slabs = ob_ref.shape[3] // LANES
    for r in range(dil):
        blk = ob_ref[0, r]
        for c in range(n_slabs):
            scr_ref[c, pl.ds(r, rows, stride=dil), :] = blk[:, c * LANES:(c + 1) * LANES]
    return [scr_ref[c] for c in range(n_slabs)]


def _post_even_kernel(final, h_ref, oa_ref, b0_ref, b1_ref, b2_ref, wo_ref, gm_ref, wu_ref, wd_ref,
                      *rest):
    gf_ref = rest[0] if final else None
    o_ref, scr1_ref, scr2_ref = rest[-3:]
    b0 = b0_ref[...]
    slabs = [[b0[:, c * LANES:(c + 1) * LANES] for c in range(2 * B_OUT // LANES)],
             _token_major_slabs(b1_ref, scr1_ref), _token_major_slabs(b2_ref, scr2_ref)]
    n_o = B_OUT // LANES
    merged = []
    for c in range(n_o):
        lses = [sl[n_o + c] for sl in slabs]
        mx = jnp.maximum(jnp.maximum(lses[0], lses[1]), lses[2])
        es = [jnp.exp(ls - mx) for ls in lses]
        num = es[0] * slabs[0][c] + es[1] * slabs[1][c] + es[2] * slabs[2][c]
        merged.append(num / (es[0] + es[1] + es[2]))
    out_b = jnp.concatenate(merged, axis=-1).astype(BF16)
    mix = jnp.dot(oa_ref[...], wo_ref[:A_OUT, :], preferred_element_type=F32)
    mix = mix + jnp.dot(out_b, wo_ref[A_OUT:, :], preferred_element_type=F32)
    _mlp_tail(h_ref[...] + mix, gm_ref, wu_ref, wd_ref, gf_ref, o_ref)


def _post_odd_kernel(final, h_ref, a_ref, wo_ref, gm_ref, wu_ref, wd_ref, *rest):
    gf_ref, o_ref = (rest[0], rest[1]) if final else (None, rest[0])
    mix = jnp.dot(a_ref[...], wo_ref[...], preferred_element_type=F32)
    _mlp_tail(h_ref[...] + mix, gm_ref, wu_ref, wd_ref, gf_ref, o_ref)


def _post(kernel_fn, name, h, acts, act_specs, scratch, wo, gm, wu, wd, gf):
    n, d = h.shape
    final = gf is not None
    params = [wo, gm, wu, wd] + ([gf] if final else [])
    return pl.pallas_call(
        functools.partial(kernel_fn, final),
        grid=(n // TM_POST,),
        in_specs=[_row_tile(d)] + act_specs + [_const_spec(p.shape) for p in params],
        out_specs=_row_tile(d),
        out_shape=jax.ShapeDtypeStruct((n, d), F32),
        scratch_shapes=scratch,
        compiler_params=pltpu.CompilerParams(dimension_semantics=("arbitrary",),
                                             vmem_limit_bytes=VMEM_LIMIT),
        name=name,
    )(h, *acts, *params)


def _row_tile(width):
    return pl.BlockSpec((TM_POST, width), lambda i: (i, 0))


def _post_even(h, oa, ob0, ob1, ob2, seq, *params):
    per_seq = seq // TM_POST
    res_spec = lambda ob: pl.BlockSpec((1, ob.shape[1], TM_POST // ob.shape[1], ob.shape[3]),
                                       lambda i: (i // per_seq, 0, i % per_seq, 0))
    specs = [_row_tile(oa.shape[1]), _row_tile(ob0.shape[1]), res_spec(ob1), res_spec(ob2)]
    scratch = [pltpu.VMEM((2 * B_OUT // LANES, TM_POST, LANES), F32)] * 2
    return _post(_post_even_kernel, "post_even", h, [oa, ob0, ob1, ob2], specs, scratch, *params)


def _post_odd(h, att, *params):
    return _post(_post_odd_kernel, "post_odd", h, [att], [_row_tile(att.shape[1])], [], *params)


def _rope_lanes(x, cos, s1, s2):
    return x * cos + pltpu.roll(x, LANES - ROPE_HALF, 1) * s1 + pltpu.roll(x, ROPE_HALF, 1) * s2


def _mla_proj_kernel(scale, h_ref, g_ref, wd_ref, qn_ref, wq_ref, kn_ref, wk_ref, wv_ref,
                     cos_ref, s1_ref, s2_ref, q_ref, k_ref, vt_ref):
    xn = _rms(h_ref[...], g_ref[...]).astype(BF16)
    down = jnp.dot(xn, wd_ref[...], preferred_element_type=F32)
    cq = _rms(down[:, :C_Q_RANK], qn_ref[...]).astype(BF16)
    ckv = _rms(down[:, C_Q_RANK:C_Q_RANK + C_KV_RANK], kn_ref[...]).astype(BF16)
    cos, s1, s2 = cos_ref[...], s1_ref[...], s2_ref[...]
    kr = _rope_lanes(down[:, C_Q_RANK + C_KV_RANK:], cos, s1, s2)
    q = jnp.dot(cq, wq_ref[...], preferred_element_type=F32)
    k = jnp.dot(ckv, wk_ref[...], preferred_element_type=F32)
    lane = lax.broadcasted_iota(jnp.int32, (1, LANES), 1)
    cos_q = jnp.where(lane < C_NOPE, 1.0, cos)
    for hh in range(C_HEADS):
        sl = slice(hh * LANES, (hh + 1) * LANES)
        q_ref[:, sl] = (_rope_lanes(q[:, sl], cos_q, s1, s2) * scale).astype(q_ref.dtype)
        k_ref[:, sl] = (k[:, sl] + kr).astype(k_ref.dtype)
    vt_ref[0] = lax.dot_general(wv_ref[...], ckv, _NT, preferred_element_type=F32).astype(vt_ref.dtype)


def _rope_tables(s):
    inv = ROPE_THETA ** (-jnp.arange(0, C_ROPE, 2, dtype=F32) / C_ROPE)
    ang = jnp.arange(s, dtype=F32)[:, None] * inv[None, :]
    cos, sin = jnp.cos(ang), jnp.sin(ang)
    z = lambda w: jnp.zeros((s, w), F32)
    tail = LANES - ROPE_LO - C_ROPE
    cos_t = jnp.concatenate([z(ROPE_LO), cos, cos, z(tail)], axis=1)
    s1_t = jnp.concatenate([z(ROPE_LO), -sin, z(ROPE_HALF), z(tail)], axis=1)
    s2_t = jnp.concatenate([z(ROPE_LO), z(ROPE_HALF), sin, z(tail)], axis=1)
    return cos_t, s1_t, s2_t


def _pad_heads(w, width):
    r = w.shape[0]
    w3 = jnp.pad(w.reshape(r, C_HEADS, width), ((0, 0), (0, 0), (0, LANES - width)))
    return w3.reshape(r, C_PAD)


def _mla_weights(w_down, w_uq, w_ukv):
    kr = jnp.zeros((D_MODEL, LANES), w_down.dtype).at[:, ROPE_LO:ROPE_LO + C_ROPE].set(
        w_down[:, C_Q_RANK + C_KV_RANK:])
    wd_ext = jnp.concatenate([w_down[:, :C_Q_RANK + C_KV_RANK], kr], axis=1)
    wq = _pad_heads(w_uq, C_NOPE + C_ROPE)
    ukv = w_ukv.reshape(C_KV_RANK, C_HEADS, C_NOPE + C_V)
    wk = _pad_heads(ukv[:, :, :C_NOPE].reshape(C_KV_RANK, -1), C_NOPE)
    wvt = ukv[:, :, C_NOPE:].reshape(C_KV_RANK, -1).T
    return wd_ext.astype(BF16), wq.astype(BF16), wk.astype(BF16), wvt.astype(BF16)


def _mla_proj(h, g, wd_ext, qn, wq, kn, wk, wvt, tables, seq):
    n, d = h.shape
    scale = (C_NOPE + C_ROPE) ** -0.5 * math.log2(math.e)
    per_seq = seq // TM_PROJ
    tab_spec = pl.BlockSpec((TM_PROJ, LANES), lambda i: (i % per_seq, 0))
    tile = lambda w: pl.BlockSpec((TM_PROJ, w), lambda i: (i, 0))
    vt_rows = C_HEADS * C_V
    vt_spec = pl.BlockSpec((1, vt_rows, TM_PROJ), lambda i: (i // per_seq, 0, i % per_seq))
    params = [g, wd_ext, qn, wq, kn, wk, wvt]
    return pl.pallas_call(
        functools.partial(_mla_proj_kernel, scale),
        grid=(n // TM_PROJ,),
        in_specs=[tile(d)] + [_const_spec(p.shape) for p in params] + [tab_spec] * 3,
        out_specs=[tile(C_PAD), tile(C_PAD), vt_spec],
        out_shape=[jax.ShapeDtypeStruct((n, C_PAD), BF16)] * 2
        + [jax.ShapeDtypeStruct((n // seq, vt_rows, seq), BF16)],
        compiler_params=pltpu.CompilerParams(dimension_semantics=("arbitrary",),
                                             vmem_limit_bytes=VMEM_LIMIT),
        name="mla_proj",
    )(h, *params, *tables)


def _mla_attn_kernel(q_ref, k_ref, vt_ref, o_ref, sa_ref, sb_ref):
    i = pl.program_id(2)
    tq, tk = T_ATT, TK_ATT
    per_q = tq // tk
    qt = q_ref[0].T
    qs = (qt[:LANES], qt[LANES:])

    def produce(chunk, buf, mask):
        off = pl.multiple_of(chunk * tk, tk)
        mx = []
        for hh in range(2):
            k = k_ref[0, pl.ds(off, tk), hh * LANES:(hh + 1) * LANES]
            s = jnp.dot(k, qs[hh], preferred_element_type=F32)
            if mask is not None:
                s = jnp.where(mask, s, NEG)
            buf[hh] = s
            mx.append(jnp.max(s, axis=0, keepdims=True))
        return tuple(mx)

    def consume(chunk, buf, mx, state):
        off = pl.multiple_of(chunk * tk, tk)
        new = []
        for hh in range(2):
            m, acc = state[hh]
            vt = vt_ref[0, hh * C_V:(hh + 1) * C_V, pl.ds(off, tk)]
            m_new = jnp.maximum(m, mx[hh])
            p = jnp.exp2(buf[hh] - m_new).astype(BF16)
            pv = jnp.dot(jnp.concatenate([vt, ones], axis=0), p, preferred_element_type=F32)
            new.append((m_new, jnp.exp2(m - m_new) * acc + pv))
        return tuple(new)

    key = lax.broadcasted_iota(jnp.int32, (tk, tq), 0)
    qry = lax.broadcasted_iota(jnp.int32, (tk, tq), 1)
    d0 = i * per_q
    ones = jnp.ones((BF16_ROWS, tk), BF16)
    head0 = (jnp.full((1, tq), NEG, F32), jnp.zeros((C_V + BF16_ROWS, tq), F32))
    mx_a = produce(d0, sa_ref, key <= qry)
    mx_b = produce(d0 + 1, sb_ref, key + tk <= qry)
    state = consume(d0, sa_ref, mx_a, (head0, head0))

    def body(jj, carry):
        mx_b, state = carry
        pend = jnp.where(jj == 0, d0 + 1, 2 * jj - 1)
        mx_a = produce(2 * jj, sa_ref, None)
        state = consume(pend, sb_ref, mx_b, state)
        mx_b = produce(2 * jj + 1, sb_ref, None)
        state = consume(2 * jj, sa_ref, mx_a, state)
        return mx_b, state

    mx_b, state = lax.fori_loop(0, i, body, (mx_b, state))
    pend = jnp.where(i == 0, d0 + 1, 2 * i - 1)
    (_, a0), (_, a1) = consume(pend, sb_ref, mx_b, state)
    o = jnp.concatenate([a[:C_V] / a[C_V:C_V + 1] for a in (a0, a1)], axis=0)
    o_ref[0] = o.T.astype(o_ref.dtype)


def _mla_attn(q, k, vt):
    bn, s, _ = q.shape
    pair = 2 * LANES
    return pl.pallas_call(
        _mla_attn_kernel,
        grid=(bn, C_HEADS // 2, s // T_ATT),
        in_specs=[pl.BlockSpec((1, T_ATT, pair), lambda b, p, i: (b, i, p)),
                  pl.BlockSpec((1, s, pair), lambda b, p, i: (b, 0, p)),
                  pl.BlockSpec((1, 2 * C_V, s), lambda b, p, i: (b, p, 0))],
        out_specs=pl.BlockSpec((1, T_ATT, 2 * C_V), lambda b, p, i: (b, i, p)),
        out_shape=jax.ShapeDtypeStruct((bn, s, C_HEADS * C_V), BF16),
        scratch_shapes=[pltpu.VMEM((2, TK_ATT, T_ATT), F32)] * 2,
        compiler_params=pltpu.CompilerParams(
            dimension_semantics=("arbitrary", "arbitrary", "arbitrary"),
            vmem_limit_bytes=VMEM_LIMIT),
        name="mla_attn",
    )(q, k, vt)


def _even_colscale():
    cs = np.ones((1, AB_IN), np.float32)
    cs[:, :A_OUT] = HEAD_DIM ** -0.5
    for g in range(len(B_BRANCHES)):
        lo = A_IN + g * B_BRANCH_IN
        cs[:, lo:lo + B_OUT] = HEAD_DIM ** -0.5
    return jnp.asarray(cs)


def kernel(x, rel_bias, attn_norm, mlp_norm, final_norm, w_in_ab, sinks, w_out_ab, w_down_c, q_norm_c,
           w_uq_c, kv_norm_c, w_ukv_c, w_o_c, w_mlp_up, w_mlp_down):
    bn, s, d = x.shape
    depth = attn_norm.shape[0]
    n = bn * s
    bias_all = _bias_table(rel_bias)
    colscale = _even_colscale()
    tables = _rope_tables(s)
    row = lambda v: v.reshape(1, -1)
    h = x.reshape(n, d)
    for layer in range(depth):
        gf = row(final_norm) if layer == depth - 1 else None
        mlp = (row(mlp_norm[layer]), w_mlp_up[layer].astype(BF16), w_mlp_down[layer].astype(BF16), gf)
        if layer % 2 == 0:
            e = layer // 2
            pa, pb0, pb1, pb2 = _even_proj(h, row(attn_norm[layer]), w_in_ab[e].astype(BF16), colscale,
                                           bn, s)
            oa = _band_a(pa.reshape(bn, s, A_IN), sinks[e], bias_all)
            ob0 = _band_b(pb0.reshape(bn, 1, s, B_BRANCH_IN), bias_all, 0)
            ob1 = _band_b(pb1, bias_all, 1)
            ob2 = _band_b(pb2, bias_all, 2)
            h = _post_even(h, oa.reshape(n, A_OUT), ob0.reshape(n, 2 * B_OUT), ob1, ob2, s,
                           w_out_ab[e].astype(BF16), *mlp)
        else:
            o = layer // 2
            wd_ext, wq, wk, wvt = _mla_weights(w_down_c[o], w_uq_c[o], w_ukv_c[o])
            q, k, vt = _mla_proj(h, row(attn_norm[layer]), wd_ext, row(q_norm_c[o]), wq,
                                 row(kv_norm_c[o]), wk, wvt, tables, s)
            att = _mla_attn(q.reshape(bn, s, C_PAD), k.reshape(bn, s, C_PAD), vt)
            h = _post_odd(h, att.reshape(n, C_HEADS * C_V), w_o_c[o].astype(BF16), *mlp)
    return h.reshape(bn, s, d)
```

```python
import functools
import math

import numpy as np
import jax
import jax.numpy as jnp
from jax import lax
from jax.experimental import pallas as pl
from jax.experimental.pallas import tpu as pltpu

F32 = jnp.float32
BF16 = jnp.bfloat16

D_MODEL = 1024
HEAD_DIM = 64
BLOCK = 128
EPS = 1e-6
NEG = -1e30
A_Q_HEADS = 8
A_KV_HEADS = 2
A_GROUP = A_Q_HEADS // A_KV_HEADS
A_WINDOW = 128
B_BRANCHES = ((128, 1), (512, 4), (2048, 16))
B_HEADS_PER_BRANCH = 4
NUM_BUCKETS = 32
MAX_DISTANCE = 2048
N_BIAS_HEADS = A_Q_HEADS + len(B_BRANCHES) * B_HEADS_PER_BRANCH
A_IN = (A_Q_HEADS + 2 * A_KV_HEADS) * HEAD_DIM
B_BRANCH_IN = 3 * B_HEADS_PER_BRANCH * HEAD_DIM
AB_IN = A_IN + len(B_BRANCHES) * B_BRANCH_IN
A_OUT = A_Q_HEADS * HEAD_DIM
B_OUT = B_HEADS_PER_BRANCH * HEAD_DIM
C_HEADS = 8
C_NOPE = 64
C_ROPE = 32
C_V = 64
C_Q_RANK = 384
C_KV_RANK = 256
ROPE_THETA = 10000.0
D_FF = 4 * D_MODEL

LANES = 128
BF16_ROWS = 16
C_PAD = C_HEADS * LANES
C_DOWN_EXT = C_Q_RANK + C_KV_RANK + LANES
ROPE_LO = C_NOPE
ROPE_HALF = C_ROPE // 2

VMEM_LIMIT = 52 * 1024 * 1024

TM_PROJ = 1024
PERM_GROUP = 256
TQ_BAND_A = 256
TQ_BAND = 512
TM_POST = 512
T_ATT = 512
TK_ATT = 256
assert T_ATT == 2 * TK_ATT

_NT = (((1,), (1,)), ((), ()))


def _const_spec(shape):
    nd = len(shape)
    return pl.BlockSpec(shape, lambda *_: (0,) * nd, pipeline_mode=pl.Buffered(1))


def _rms(x, g):
    ms = jnp.mean(x * x, axis=-1, keepdims=True)
    return x * lax.rsqrt(ms + EPS) * g


def _t5_bucket_np(n):
    max_exact = NUM_BUCKETS // 2
    nf = np.maximum(n, 1).astype(np.float32)
    large = max_exact + (np.log(nf / max_exact) / math.log(MAX_DISTANCE / max_exact)
                         * (NUM_BUCKETS - max_exact)).astype(np.int32)
    return np.where(n < max_exact, n, np.minimum(large, NUM_BUCKETS - 1))


def _band_bucket_index(dilation, max_dist):
    i = np.arange(BLOCK)[:, None]
    j = np.arange(2 * BLOCK)[None, :]
    dist = i + BLOCK - j
    inband = (dist >= 0) & (dist <= max_dist)
    return np.where(inband, _t5_bucket_np(np.maximum(dist, 0) * dilation), -1).astype(np.int32)


def _bias_kernel(tab_ref, idx_ref, o_ref):
    h = pl.program_id(0)
    idx = idx_ref[0]
    acc = jnp.full(idx.shape, NEG, F32)
    for b in range(NUM_BUCKETS):
        acc = jnp.where(idx == b, tab_ref[h * NUM_BUCKETS + b], acc)
    o_ref[0] = acc


def _bias_table(rel_bias):
    configs = [(1, A_WINDOW - 1)] + [(dil, window // dil) for window, dil in B_BRANCHES]
    idx = jnp.asarray(np.stack([_band_bucket_index(d, m) for d, m in configs]))
    tab = rel_bias.T.reshape(-1)

    def cfg(h):
        return jnp.where(h < A_Q_HEADS, 0, 1 + (h - A_Q_HEADS) // B_HEADS_PER_BRANCH)

    return pl.pallas_call(
        _bias_kernel,
        grid=(N_BIAS_HEADS,),
        in_specs=[pl.BlockSpec(memory_space=pltpu.SMEM),
                  pl.BlockSpec((1, BLOCK, 2 * BLOCK), lambda h: (cfg(h), 0, 0))],
        out_specs=pl.BlockSpec((1, BLOCK, 2 * BLOCK), lambda h: (h, 0, 0)),
        out_shape=jax.ShapeDtypeStruct((N_BIAS_HEADS, BLOCK, 2 * BLOCK), F32),
        name="bias_table",
    )(tab, idx)


def _deinterleave_matrix(dil):
    rho = np.arange(PERM_GROUP)
    per_res = PERM_GROUP // dil
    p = np.zeros((PERM_GROUP, PERM_GROUP), np.float32)
    p[rho, (rho % per_res) * dil + rho // per_res] = 1.0
    return jnp.asarray(p, BF16)


def _even_proj_kernel(x_ref, g_ref, w_ref, cs_ref, p1_ref, p2_ref, pa_ref, pb0_ref, pb1_ref, pb2_ref):
    xn = _rms(x_ref[...], g_ref[...]).astype(BF16)
    y = (jnp.dot(xn, w_ref[...], preferred_element_type=F32) * cs_ref[...]).astype(BF16)
    pa_ref[...] = y[:, :A_IN]
    pb0_ref[...] = y[:, A_IN:A_IN + B_BRANCH_IN]
    for g, (perm_ref, out_ref) in enumerate(((p1_ref, pb1_ref), (p2_ref, pb2_ref)), start=1):
        dil = B_BRANCHES[g][1]
        rows = PERM_GROUP // dil
        lo = A_IN + g * B_BRANCH_IN
        for grp in range(TM_PROJ // PERM_GROUP):
            yg = y[grp * PERM_GROUP:(grp + 1) * PERM_GROUP, lo:lo + B_BRANCH_IN]
            z = jnp.dot(perm_ref[...], yg, preferred_element_type=F32).astype(BF16)
            for r in range(dil):
                out_ref[0, r, grp * rows:(grp + 1) * rows, :] = z[r * rows:(r + 1) * rows]


def _even_proj(h, g, w, colscale, bn, seq):
    n, d = h.shape
    per_seq = seq // TM_PROJ
    tile = lambda w_: pl.BlockSpec((TM_PROJ, w_), lambda i: (i, 0))
    dils = [dil for _, dil in B_BRANCHES[1:]]
    dil_spec = lambda dil: pl.BlockSpec((1, dil, TM_PROJ // dil, B_BRANCH_IN),
                                        lambda i: (i // per_seq, 0, i % per_seq, 0))
    perms = [_deinterleave_matrix(dil) for dil in dils]
    return pl.pallas_call(
        _even_proj_kernel,
        grid=(n // TM_PROJ,),
        in_specs=[tile(d), _const_spec((1, d)), _const_spec(w.shape), _const_spec(colscale.shape)]
        + [_const_spec(p.shape) for p in perms],
        out_specs=[tile(A_IN), tile(B_BRANCH_IN)] + [dil_spec(dil) for dil in dils],
        out_shape=[jax.ShapeDtypeStruct((n, A_IN), BF16), jax.ShapeDtypeStruct((n, B_BRANCH_IN), BF16)]
        + [jax.ShapeDtypeStruct((bn, dil, seq // dil, B_BRANCH_IN), BF16) for dil in dils],
        compiler_params=pltpu.CompilerParams(dimension_semantics=("arbitrary",),
                                             vmem_limit_bytes=VMEM_LIMIT),
        name="even_proj",
    )(h, g, w, colscale, *perms)


def _band_head(q, k, v, bias, first_mask, sink):
    s = lax.dot_general(q, k, _NT, preferred_element_type=F32) + bias
    if first_mask is not None:
        s = jnp.where(first_mask, NEG, s)
    m = jnp.max(s, axis=-1, keepdims=True)
    if sink is not None:
        m = jnp.maximum(m, sink)
    p = jnp.exp(s - m)
    l = jnp.sum(p, axis=-1, keepdims=True)
    if sink is not None:
        l = l + jnp.exp(sink - m)
    o = jnp.dot(p.astype(BF16), v, preferred_element_type=F32)
    return o / l, m, l


def _first_block_mask(is_first_tile):
    col = lax.broadcasted_iota(jnp.int32, (BLOCK, 2 * BLOCK), 1)
    return jnp.logical_and(is_first_tile, col < BLOCK)


def _with_prev(cur, prev, lo, width):
    return jnp.concatenate([prev[:, lo:lo + width], cur[:, lo:lo + width]], axis=0)


def _band_a_kernel(sink_ref, cur_ref, prev_ref, bias_ref, o_ref):
    is_first = pl.program_id(1) == 0
    cur, prev = cur_ref[0], prev_ref[0]
    kd = A_KV_HEADS * HEAD_DIM
    q_heads = [cur[:, h * HEAD_DIM:(h + 1) * HEAD_DIM] for h in range(A_Q_HEADS)]
    kv_heads = [(_with_prev(cur, prev, A_OUT + g * HEAD_DIM, HEAD_DIM),
                 _with_prev(cur, prev, A_OUT + kd + g * HEAD_DIM, HEAD_DIM)) for g in range(A_KV_HEADS)]
    for j in range(TQ_BAND_A // BLOCK):
        rows = slice(j * BLOCK, (j + 1) * BLOCK)
        krows = slice(j * BLOCK, (j + 2) * BLOCK)
        mask = _first_block_mask(is_first) if j == 0 else None
        outs = []
        for h in range(A_Q_HEADS):
            k_g, v_g = kv_heads[h // A_GROUP]
            o, _, _ = _band_head(q_heads[h][rows], k_g[krows], v_g[krows], bias_ref[h], mask, sink_ref[h])
            outs.append(o)
        o_ref[0, rows, :] = jnp.concatenate(outs, axis=-1).astype(o_ref.dtype)


def _band_a(pa, sinks, bias_all):
    bn, s, _ = pa.shape
    per_tile = TQ_BAND_A // BLOCK
    return pl.pallas_call(
        _band_a_kernel,
        grid=(bn, s // TQ_BAND_A),
        in_specs=[pl.BlockSpec(memory_space=pltpu.SMEM),
                  pl.BlockSpec((1, TQ_BAND_A, A_IN), lambda b, n: (b, n, 0)),
                  pl.BlockSpec((1, BLOCK, A_IN), lambda b, n: (b, jnp.maximum(n * per_tile - 1, 0), 0)),
                  pl.BlockSpec((A_Q_HEADS, BLOCK, 2 * BLOCK), lambda b, n: (0, 0, 0))],
        out_specs=pl.BlockSpec((1, TQ_BAND_A, A_OUT), lambda b, n: (b, n, 0)),
        out_shape=jax.ShapeDtypeStruct((bn, s, A_OUT), BF16),
        compiler_params=pltpu.CompilerParams(dimension_semantics=("arbitrary", "arbitrary")),
        name="band_a",
    )(sinks, pa, pa, bias_all)


def _band_b_kernel(cur_ref, prev_ref, bias_ref, o_ref):
    is_first = pl.program_id(2) == 0
    cur, prev = cur_ref[0, 0], prev_ref[0, 0]
    heads = []
    for h in range(B_HEADS_PER_BRANCH):
        lo = h * HEAD_DIM
        heads.append((cur[:, lo:lo + HEAD_DIM], _with_prev(cur, prev, B_OUT + lo, HEAD_DIM),
                      _with_prev(cur, prev, 2 * B_OUT + lo, HEAD_DIM)))
    for j in range(TQ_BAND // BLOCK):
        rows = slice(j * BLOCK, (j + 1) * BLOCK)
        krows = slice(j * BLOCK, (j + 2) * BLOCK)
        mask = _first_block_mask(is_first) if j == 0 else None
        outs, lses = [], []
        for h, (q_h, k_h, v_h) in enumerate(heads):
            o, m, l = _band_head(q_h[rows], k_h[krows], v_h[krows], bias_ref[h], mask, None)
            outs.append(o)
            lses.append(jnp.broadcast_to(m + jnp.log(l), (BLOCK, HEAD_DIM)))
        o_ref[0, 0, rows, :] = jnp.concatenate(outs + lses, axis=-1)


def _band_b(pb, bias_all, g):
    bn, dil, l, _ = pb.shape
    per_tile = TQ_BAND // BLOCK
    bias_blk = (A_Q_HEADS + g * B_HEADS_PER_BRANCH) // B_HEADS_PER_BRANCH
    return pl.pallas_call(
        _band_b_kernel,
        grid=(bn, dil, l // TQ_BAND),
        in_specs=[pl.BlockSpec((1, 1, TQ_BAND, B_BRANCH_IN), lambda b, r, n: (b, r, n, 0)),
                  pl.BlockSpec((1, 1, BLOCK, B_BRANCH_IN),
                               lambda b, r, n: (b, r, jnp.maximum(n * per_tile - 1, 0), 0)),
                  pl.BlockSpec((B_HEADS_PER_BRANCH, BLOCK, 2 * BLOCK), lambda b, r, n: (bias_blk, 0, 0))],
        out_specs=pl.BlockSpec((1, 1, TQ_BAND, 2 * B_OUT), lambda b, r, n: (b, r, n, 0)),
        out_shape=jax.ShapeDtypeStruct((bn, dil, l, 2 * B_OUT), F32),
        compiler_params=pltpu.CompilerParams(
            dimension_semantics=("arbitrary", "arbitrary", "arbitrary")),
        name=f"band_b{g}",
    )(pb, pb, bias_all)


def _mlp_tail(h2, gm_ref, wu_ref, wd_ref, gf_ref, o_ref):
    xn = _rms(h2, gm_ref[...]).astype(BF16)
    u = jnp.maximum(jnp.dot(xn, wu_ref[...], preferred_element_type=F32), 0.0)
    u = (u * u).astype(BF16)
    h3 = h2 + jnp.dot(u, wd_ref[...], preferred_element_type=F32)
    if gf_ref is not None:
        h3 = _rms(h3, gf_ref[...])
    o_ref[...] = h3


def _token_major_slabs(ob_ref, scr_ref):
    dil = ob_ref.shape[1]
    rows = TM_POST // dil
    n_slabs = ob_ref.shape[3] // LANES
    for r in range(dil):
        blk = ob_ref[0, r]
        for c in range(n_slabs):
            scr_ref[c, pl.ds(r, rows, stride=dil), :] = blk[:, c * LANES:(c + 1) * LANES]
    return [scr_ref[c] for c in range(n_slabs)]


def _post_even_kernel(final, h_ref, oa_ref, b0_ref, b1_ref, b2_ref, wo_ref, gm_ref, wu_ref, wd_ref,
                      *rest):
    gf_ref = rest[0] if final else None
    o_ref, scr1_ref, scr2_ref = rest[-3:]
    b0 = b0_ref[...]
    slabs = [[b0[:, c * LANES:(c + 1) * LANES] for c in range(2 * B_OUT // LANES)],
             _token_major_slabs(b1_ref, scr1_ref), _token_major_slabs(b2_ref, scr2_ref)]
    n_o = B_OUT // LANES
    merged = []
    for c in range(n_o):
        lses = [sl[n_o + c] for sl in slabs]
        mx = jnp.maximum(jnp.maximum(lses[0], lses[1]), lses[2])
        es = [jnp.exp(ls - mx) for ls in lses]
        num = es[0] * slabs[0][c] + es[1] * slabs[1][c] + es[2] * slabs[2][c]
        merged.append(num / (es[0] + es[1] + es[2]))
    out_b = jnp.concatenate(merged, axis=-1).astype(BF16)
    mix = jnp.dot(oa_ref[...], wo_ref[:A_OUT, :], preferred_element_type=F32)
    mix = mix + jnp.dot(out_b, wo_ref[A_OUT:, :], preferred_element_type=F32)
    _mlp_tail(h_ref[...] + mix, gm_ref, wu_ref, wd_ref, gf_ref, o_ref)


def _post_odd_kernel(final, h_ref, a_ref, wo_ref, gm_ref, wu_ref, wd_ref, *rest):
    gf_ref, o_ref = (rest[0], rest[1]) if final else (None, rest[0])
    mix = jnp.dot(a_ref[...], wo_ref[...], preferred_element_type=F32)
    _mlp_tail(h_ref[...] + mix, gm_ref, wu_ref, wd_ref, gf_ref, o_ref)


def _post(kernel_fn, name, h, acts, act_specs, scratch, wo, gm, wu, wd, gf):
    n, d = h.shape
    final = gf is not None
    params = [wo, gm, wu, wd] + ([gf] if final else [])
    return pl.pallas_call(
        functools.partial(kernel_fn, final),
        grid=(n // TM_POST,),
        in_specs=[_row_tile(d)] + act_specs + [_const_spec(p.shape) for p in params],
        out_specs=_row_tile(d),
        out_shape=jax.ShapeDtypeStruct((n, d), F32),
        scratch_shapes=scratch,
        compiler_params=pltpu.CompilerParams(dimension_semantics=("arbitrary",),
                                             vmem_limit_bytes=VMEM_LIMIT),
        name=name,
    )(h, *acts, *params)


def _row_tile(width):
    return pl.BlockSpec((TM_POST, width), lambda i: (i, 0))


def _post_even(h, oa, ob0, ob1, ob2, seq, *params):
    per_seq = seq // TM_POST
    res_spec = lambda ob: pl.BlockSpec((1, ob.shape[1], TM_POST // ob.shape[1], ob.shape[3]),
                                       lambda i: (i // per_seq, 0, i % per_seq, 0))
    specs = [_row_tile(oa.shape[1]), _row_tile(ob0.shape[1]), res_spec(ob1), res_spec(ob2)]
    scratch = [pltpu.VMEM((2 * B_OUT // LANES, TM_POST, LANES), F32)] * 2
    return _post(_post_even_kernel, "post_even", h, [oa, ob0, ob1, ob2], specs, scratch, *params)


def _post_odd(h, att, *params):
    return _post(_post_odd_kernel, "post_odd", h, [att], [_row_tile(att.shape[1])], [], *params)


def _rope_lanes(x, cos, s1, s2):
    return x * cos + pltpu.roll(x, LANES - ROPE_HALF, 1) * s1 + pltpu.roll(x, ROPE_HALF, 1) * s2


def _mla_proj_kernel(scale, h_ref, g_ref, wd_ref, qn_ref, wq_ref, kn_ref, wk_ref, wv_ref,
                     cos_ref, s1_ref, s2_ref, q_ref, k_ref, vt_ref):
    xn = _rms(h_ref[...], g_ref[...]).astype(BF16)
    down = jnp.dot(xn, wd_ref[...], preferred_element_type=F32)
    cq = _rms(down[:, :C_Q_RANK], qn_ref[...]).astype(BF16)
    ckv = _rms(down[:, C_Q_RANK:C_Q_RANK + C_KV_RANK], kn_ref[...]).astype(BF16)
    cos, s1, s2 = cos_ref[...], s1_ref[...], s2_ref[...]
    kr = _rope_lanes(down[:, C_Q_RANK + C_KV_RANK:], cos, s1, s2)
    q = jnp.dot(cq, wq_ref[...], preferred_element_type=F32)
    k = jnp.dot(ckv, wk_ref[...], preferred_element_type=F32)
    lane = lax.broadcasted_iota(jnp.int32, (1, LANES), 1)
    cos_q = jnp.where(lane < C_NOPE, 1.0, cos)
    for hh in range(C_HEADS):
        sl = slice(hh * LANES, (hh + 1) * LANES)
        q_ref[:, sl] = (_rope_lanes(q[:, sl], cos_q, s1, s2) * scale).astype(q_ref.dtype)
        k_ref[:, sl] = (k[:, sl] + kr).astype(k_ref.dtype)
    vt_ref[0] = lax.dot_general(wv_ref[...], ckv, _NT, preferred_element_type=F32).astype(vt_ref.dtype)


def _rope_tables(s):
    inv = ROPE_THETA ** (-jnp.arange(0, C_ROPE, 2, dtype=F32) / C_ROPE)
    ang = jnp.arange(s, dtype=F32)[:, None] * inv[None, :]
    cos, sin = jnp.cos(ang), jnp.sin(ang)
    z = lambda w: jnp.zeros((s, w), F32)
    tail = LANES - ROPE_LO - C_ROPE
    cos_t = jnp.concatenate([z(ROPE_LO), cos, cos, z(tail)], axis=1)
    s1_t = jnp.concatenate([z(ROPE_LO), -sin, z(ROPE_HALF), z(tail)], axis=1)
    s2_t = jnp.concatenate([z(ROPE_LO), z(ROPE_HALF), sin, z(tail)], axis=1)
    return cos_t, s1_t, s2_t


def _pad_heads(w, width):
    r = w.shape[0]
    w3 = jnp.pad(w.reshape(r, C_HEADS, width), ((0, 0), (0, 0), (0, LANES - width)))
    return w3.reshape(r, C_PAD)


def _mla_weights(w_down, w_uq, w_ukv):
    kr = jnp.zeros((D_MODEL, LANES), w_down.dtype).at[:, ROPE_LO:ROPE_LO + C_ROPE].set(
        w_down[:, C_Q_RANK + C_KV_RANK:])
    wd_ext = jnp.concatenate([w_down[:, :C_Q_RANK + C_KV_RANK], kr], axis=1)
    wq = _pad_heads(w_uq, C_NOPE + C_ROPE)
    ukv = w_ukv.reshape(C_KV_RANK, C_HEADS, C_NOPE + C_V)
    wk = _pad_heads(ukv[:, :, :C_NOPE].reshape(C_KV_RANK, -1), C_NOPE)
    wvt = ukv[:, :, C_NOPE:].reshape(C_KV_RANK, -1).T
    return wd_ext.astype(BF16), wq.astype(BF16), wk.astype(BF16), wvt.astype(BF16)


def _mla_proj(h, g, wd_ext, qn, wq, kn, wk, wvt, tables, seq):
    n, d = h.shape
    scale = (C_NOPE + C_ROPE) ** -0.5 * math.log2(math.e)
    per_seq = seq // TM_PROJ
    tab_spec = pl.BlockSpec((TM_PROJ, LANES), lambda i: (i % per_seq, 0))
    tile = lambda w: pl.BlockSpec((TM_PROJ, w), lambda i: (i, 0))
    vt_rows = C_HEADS * C_V
    vt_spec = pl.BlockSpec((1, vt_rows, TM_PROJ), lambda i: (i // per_seq, 0, i % per_seq))
    params = [g, wd_ext, qn, wq, kn, wk, wvt]
    return pl.pallas_call(
        functools.partial(_mla_proj_kernel, scale),
        grid=(n // TM_PROJ,),
        in_specs=[tile(d)] + [_const_spec(p.shape) for p in params] + [tab_spec] * 3,
        out_specs=[tile(C_PAD), tile(C_PAD), vt_spec],
        out_shape=[jax.ShapeDtypeStruct((n, C_PAD), BF16)] * 2
        + [jax.ShapeDtypeStruct((n // seq, vt_rows, seq), BF16)],
        compiler_params=pltpu.CompilerParams(dimension_semantics=("arbitrary",),
                                             vmem_limit_bytes=VMEM_LIMIT),
        name="mla_proj",
    )(h, *params, *tables)


def _post_even_mla_kernel(scale, h_ref, oa_ref, b0_ref, b1_ref, b2_ref, wo_ref, gm_ref, wu_ref, wd_ref,
                          *rest):
    mla_refs, (o_ref, q_ref, k_ref, vt_ref, scr1_ref, scr2_ref) = rest[:10], rest[10:]
    _post_even_kernel(False, h_ref, oa_ref, b0_ref, b1_ref, b2_ref, wo_ref, gm_ref, wu_ref, wd_ref,
                      o_ref, scr1_ref, scr2_ref)
    _mla_proj_kernel(scale, o_ref, *mla_refs, q_ref, k_ref, vt_ref)


def _post_even_mla(h, oa, ob0, ob1, ob2, seq, post_params, mla_params, tables):
    n, d = h.shape
    scale = (C_NOPE + C_ROPE) ** -0.5 * math.log2(math.e)
    per_seq = seq // TM_POST
    res_spec = lambda ob: pl.BlockSpec((1, ob.shape[1], TM_POST // ob.shape[1], ob.shape[3]),
                                       lambda i: (i // per_seq, 0, i % per_seq, 0))
    act_specs = [_row_tile(oa.shape[1]), _row_tile(ob0.shape[1]), res_spec(ob1), res_spec(ob2)]
    tab_spec = pl.BlockSpec((TM_POST, LANES), lambda i: (i % per_seq, 0))
    vt_rows = C_HEADS * C_V
    vt_spec = pl.BlockSpec((1, vt_rows, TM_POST), lambda i: (i // per_seq, 0, i % per_seq))
    params = list(post_params) + list(mla_params)
    return pl.pallas_call(
        functools.partial(_post_even_mla_kernel, scale),
        grid=(n // TM_POST,),
        in_specs=[_row_tile(d)] + act_specs + [_const_spec(p.shape) for p in params] + [tab_spec] * 3,
        out_specs=[_row_tile(d), _row_tile(C_PAD), _row_tile(C_PAD), vt_spec],
        out_shape=[jax.ShapeDtypeStruct((n, d), F32)] + [jax.ShapeDtypeStruct((n, C_PAD), BF16)] * 2
        + [jax.ShapeDtypeStruct((n // seq, vt_rows, seq), BF16)],
        scratch_shapes=[pltpu.VMEM((2 * B_OUT // LANES, TM_POST, LANES), F32)] * 2,
        compiler_params=pltpu.CompilerParams(dimension_semantics=("arbitrary",),
                                             vmem_limit_bytes=VMEM_LIMIT),
        name="post_even_mla",
    )(h, oa, ob0, ob1, ob2, *params, *tables)


def _mla_attn_kernel(q_ref, k_ref, vt_ref, o_ref, sa_ref, sb_ref):
    i = pl.program_id(2)
    tq, tk = T_ATT, TK_ATT
    per_q = tq // tk
    qt = q_ref[0].T
    qs = (qt[:LANES], qt[LANES:])

    def produce(chunk, buf, mask):
        off = pl.multiple_of(chunk * tk, tk)
        mx = []
        for hh in range(2):
            k = k_ref[0, pl.ds(off, tk), hh * LANES:(hh + 1) * LANES]
            s = jnp.dot(k, qs[hh], preferred_element_type=F32)
            if mask is not None:
                s = jnp.where(mask, s, NEG)
            buf[hh] = s
            mx.append(jnp.max(s, axis=0, keepdims=True))
        return tuple(mx)

    def consume(chunk, buf, mx, state):
        off = pl.multiple_of(chunk * tk, tk)
        new = []
        for hh in range(2):
            m, acc = state[hh]
            vt = vt_ref[0, hh * C_V:(hh + 1) * C_V, pl.ds(off, tk)]
            m_new = jnp.maximum(m, mx[hh])
            p = jnp.exp2(buf[hh] - m_new).astype(BF16)
            pv = jnp.dot(jnp.concatenate([vt, ones], axis=0), p, preferred_element_type=F32)
            new.append((m_new, jnp.exp2(m - m_new) * acc + pv))
        return tuple(new)

    key = lax.broadcasted_iota(jnp.int32, (tk, tq), 0)
    qry = lax.broadcasted_iota(jnp.int32, (tk, tq), 1)
    d0 = i * per_q
    ones = jnp.ones((BF16_ROWS, tk), BF16)
    head0 = (jnp.full((1, tq), NEG, F32), jnp.zeros((C_V + BF16_ROWS, tq), F32))
    mx_a = produce(d0, sa_ref, key <= qry)
    mx_b = produce(d0 + 1, sb_ref, key + tk <= qry)
    state = consume(d0, sa_ref, mx_a, (head0, head0))

    def body(jj, carry):
        mx_b, state = carry
        pend = jnp.where(jj == 0, d0 + 1, 2 * jj - 1)
        mx_a = produce(2 * jj, sa_ref, None)
        state = consume(pend, sb_ref, mx_b, state)
        mx_b = produce(2 * jj + 1, sb_ref, None)
        state = consume(2 * jj, sa_ref, mx_a, state)
        return mx_b, state

    mx_b, state = lax.fori_loop(0, i, body, (mx_b, state))
    pend = jnp.where(i == 0, d0 + 1, 2 * i - 1)
    (_, a0), (_, a1) = consume(pend, sb_ref, mx_b, state)
    o = jnp.concatenate([a[:C_V] / a[C_V:C_V + 1] for a in (a0, a1)], axis=0)
    o_ref[0] = o.T.astype(o_ref.dtype)


def _mla_attn(q, k, vt):
    bn, s, _ = q.shape
    pair = 2 * LANES
    return pl.pallas_call(
        _mla_attn_kernel,
        grid=(bn, C_HEADS // 2, s // T_ATT),
        in_specs=[pl.BlockSpec((1, T_ATT, pair), lambda b, p, i: (b, i, p)),
                  pl.BlockSpec((1, s, pair), lambda b, p, i: (b, 0, p)),
                  pl.BlockSpec((1, 2 * C_V, s), lambda b, p, i: (b, p, 0))],
        out_specs=pl.BlockSpec((1, T_ATT, 2 * C_V), lambda b, p, i: (b, i, p)),
        out_shape=jax.ShapeDtypeStruct((bn, s, C_HEADS * C_V), BF16),
        scratch_shapes=[pltpu.VMEM((2, TK_ATT, T_ATT), F32)] * 2,
        compiler_params=pltpu.CompilerParams(
            dimension_semantics=("arbitrary", "arbitrary", "arbitrary"),
            vmem_limit_bytes=VMEM_LIMIT),
        name="mla_attn",
    )(q, k, vt)


def _even_colscale():
    cs = np.ones((1, AB_IN), np.float32)
    cs[:, :A_OUT] = HEAD_DIM ** -0.5
    for g in range(len(B_BRANCHES)):
        lo = A_IN + g * B_BRANCH_IN
        cs[:, lo:lo + B_OUT] = HEAD_DIM ** -0.5
    return jnp.asarray(cs)


def kernel(x, rel_bias, attn_norm, mlp_norm, final_norm, w_in_ab, sinks, w_out_ab, w_down_c, q_norm_c,
           w_uq_c, kv_norm_c, w_ukv_c, w_o_c, w_mlp_up, w_mlp_down):
    bn, s, d = x.shape
    depth = attn_norm.shape[0]
    n = bn * s
    bias_all = _bias_table(rel_bias)
    colscale = _even_colscale()
    tables = _rope_tables(s)
    row = lambda v: v.reshape(1, -1)
    h = x.reshape(n, d)
    for layer in range(depth):
        gf = row(final_norm) if layer == depth - 1 else None
        mlp = (row(mlp_norm[layer]), w_mlp_up[layer].astype(BF16), w_mlp_down[layer].astype(BF16), gf)
        if layer % 2 == 0:
            e = layer // 2
            pa, pb0, pb1, pb2 = _even_proj(h, row(attn_norm[layer]), w_in_ab[e].astype(BF16), colscale,
                                           bn, s)
            oa = _band_a(pa.reshape(bn, s, A_IN), sinks[e], bias_all)
            ob0 = _band_b(pb0.reshape(bn, 1, s, B_BRANCH_IN), bias_all, 0)
            ob1 = _band_b(pb1, bias_all, 1)
            ob2 = _band_b(pb2, bias_all, 2)
            acts = (oa.reshape(n, A_OUT), ob0.reshape(n, 2 * B_OUT), ob1, ob2, s)
            if layer + 1 < depth:
                o = (layer + 1) // 2
                wd_ext, wq, wk, wvt = _mla_weights(w_down_c[o], w_uq_c[o], w_ukv_c[o])
                h, q, k, vt = _post_even_mla(
                    h, *acts, (w_out_ab[e].astype(BF16),) + mlp[:3],
                    (row(attn_norm[layer + 1]), wd_ext, row(q_norm_c[o]), wq, row(kv_norm_c[o]), wk, wvt),
                    tables)
            else:
                h = _post_even(h, *acts, w_out_ab[e].astype(BF16), *mlp)
        else:
            o = layer // 2
            att = _mla_attn(q.reshape(bn, s, C_PAD), k.reshape(bn, s, C_PAD), vt)
            h = _post_odd(h, att.reshape(n, C_HEADS * C_V), w_o_c[o].astype(BF16), *mlp)
    return h.reshape(bn, s, d)
```
